```python
import jax, jax.numpy as jnp
from jax import lax
import numpy as np

D_MODEL = 2048
BATCH = 2
SEQ = 4096
DEPTH = 2
DEC_BATCH = 32
DEC_SEQ = 1
PAST_LEN = 8192
PAGE_SIZE = 128

HEAD_DIM = 128
HEADS_PER_GROUP = 4
GROUPS = ((128, 1), (512, 4), (2048, 16))
N_GROUPS = len(GROUPS)
N_ATT_HEADS = N_GROUPS * HEADS_PER_GROUP
ATT_W = N_ATT_HEADS * HEAD_DIM
ATT_OUT = HEADS_PER_GROUP * HEAD_DIM
BLOCK = 128
ROT_DIM = HEAD_DIM // 4
ROPE_THETA = 500000.0
CONV_CH = 3 * D_MODEL // 4
CONV_WIDTH = 31
X_HEADS = 4
X_HEAD_DIM = D_MODEL // 8
X_W = X_HEADS * X_HEAD_DIM
N_MEM = 256
D_FF = 256 * ((8 * D_MODEL // 3 + 255) // 256)
N_BRANCH = 3
IN_SPLITS = (ATT_W, 2 * ATT_W, 3 * ATT_W, 3 * ATT_W + 2 * CONV_CH, 3 * ATT_W + 2 * CONV_CH + X_W)
IN_W = IN_SPLITS[-1] + N_BRANCH * D_MODEL
EPS = 1e-6
NEG = -1e30

kernel_name = 'hybrid_conformer_dilated_decoder_step'


def rms_norm(x, g):
    xf = x.astype(jnp.float32)
    y = xf * lax.rsqrt(jnp.mean(xf * xf, axis=-1, keepdims=True) + EPS)
    return (y * g.astype(jnp.float32)).astype(x.dtype)


def layer_norm(x, g, b):
    xf = x.astype(jnp.float32)
    mu = jnp.mean(xf, axis=-1, keepdims=True)
    xc = xf - mu
    y = xc * lax.rsqrt(jnp.mean(xc * xc, axis=-1, keepdims=True) + EPS)
    return (y * g.astype(jnp.float32) + b.astype(jnp.float32)).astype(x.dtype)


def swiglu(x, w_gate, w_up, w_down):
    return (jax.nn.silu(x @ w_gate) * (x @ w_up)) @ w_down


def rope_partial(x, pos):
    half = ROT_DIM // 2
    inv = ROPE_THETA ** (-jnp.arange(half, dtype=jnp.float32) / half)
    ang = pos.astype(jnp.float32)[:, None] * inv[None, :]
    cos, sin = jnp.cos(ang)[:, None, :], jnp.sin(ang)[:, None, :]
    xr = x[..., :ROT_DIM].astype(jnp.float32)
    x1, x2 = xr[..., :half], xr[..., half:]
    rot = jnp.concatenate([x1 * cos - x2 * sin, x2 * cos + x1 * sin], axis=-1)
    return jnp.concatenate([rot.astype(x.dtype), x[..., ROT_DIM:]], axis=-1)


def dilated_group_prompt(q, k, v, window, dil):
    B, S, H, Dh = q.shape
    span = window // dil
    L = S // dil
    nb = -(-L // BLOCK)
    Lp = nb * BLOCK

    def to_blocks(t):
        t = t.reshape(B, L, dil, H, Dh).transpose(0, 2, 1, 3, 4)
        t = jnp.pad(t, ((0, 0), (0, 0), (0, Lp - L), (0, 0), (0, 0)))
        return t.reshape(B, dil, nb, BLOCK, H, Dh)

    def band(t):
        prev = jnp.pad(t, ((0, 0), (0, 0), (1, 0), (0, 0), (0, 0), (0, 0)))[:, :, :-1]
        return jnp.concatenate([prev, t], axis=3)

    qb = to_blocks(q)
    kb = band(to_blocks(k))
    vb = band(to_blocks(v))
    s = jnp.einsum('brnqhd,brnkhd->brnhqk', qb, kb, preferred_element_type=jnp.float32) * (Dh ** -0.5)
    ki = jnp.arange(2 * BLOCK)[None, :]
    dist = (jnp.arange(BLOCK)[:, None] + BLOCK) - ki
    kabs = jnp.arange(nb)[:, None, None] * BLOCK - BLOCK + ki[None]
    mask = ((dist >= 0) & (dist <= span))[None] & (kabs >= 0)
    s = jnp.where(mask[None, None, :, None], s, NEG)
    m = jnp.max(s, axis=-1, keepdims=True)
    p = jnp.exp(s - m)
    den = jnp.sum(p, axis=-1, keepdims=True)
    o = jnp.einsum('brnhqk,brnkhd->brnqhd', p / den, vb.astype(jnp.float32))
    lse = (m + jnp.log(den))[..., 0].transpose(0, 1, 2, 4, 3)
    o = o.reshape(B, dil, Lp, H, Dh)[:, :, :L].transpose(0, 2, 1, 3, 4).reshape(B, S, H, Dh)
    lse = lse.reshape(B, dil, Lp, H)[:, :, :L].transpose(0, 2, 1, 3).reshape(B, S, H)
    return o, lse


def dilated_group_sample(q, k, v, buf, window, dil):
    Lb, T, Dh = buf.shape[1], q.shape[1], q.shape[-1]
    ext = jnp.concatenate([buf.astype(k.dtype), jnp.stack([k, v], axis=2)], axis=1)
    span = window // dil
    idx = Lb + jnp.arange(T)[:, None] - dil * jnp.arange(span + 1)[None, :]
    valid = idx >= 0
    g = ext[:, jnp.maximum(idx, 0)]
    s = jnp.einsum('bthd,btkhd->bthk', q, g[:, :, :, 0], preferred_element_type=jnp.float32) * (Dh ** -0.5)
    s = jnp.where(valid[None, :, None, :], s, NEG)
    m = jnp.max(s, axis=-1, keepdims=True)
    p = jnp.exp(s - m)
    den = jnp.sum(p, axis=-1, keepdims=True)
    o = jnp.einsum('bthk,btkhd->bthd', p / den, g[:, :, :, 1].astype(jnp.float32))
    lse = (m + jnp.log(den))[..., 0]
    return o, lse, ext[:, -Lb:]


def combine_groups(outs, lses):
    w = jax.nn.softmax(jnp.stack(lses, axis=0), axis=0)
    return jnp.einsum('gbth,gbthd->bthd', w, jnp.stack(outs, axis=0))


def attn_prompt(q, k, v):
    S = q.shape[1]
    outs, lses, rows = [], [], []
    for gi, (win, dil) in enumerate(GROUPS):
        o, lse = dilated_group_prompt(q[:, :, gi], k[:, :, gi], v[:, :, gi], win, dil)
        outs.append(o)
        lses.append(lse)
        keep = min(win, S)
        rows.append(jnp.stack([k[:, S - keep:, gi], v[:, S - keep:, gi]], axis=2))
    return combine_groups(outs, lses).astype(q.dtype), rows


def make_attn_sample(bufs):
    def attn(q, k, v):
        outs, lses, rows = [], [], []
        for gi, (win, dil) in enumerate(GROUPS):
            o, lse, nbuf = dilated_group_sample(q[:, :, gi], k[:, :, gi], v[:, :, gi], bufs[gi], win, dil)
            outs.append(o)
            lses.append(lse)
            rows.append(nbuf)
        return combine_groups(outs, lses).astype(q.dtype), rows
    return attn


def depthwise_conv_valid(ext, w, b):
    out = lax.conv_general_dilated(ext, w[:, None, :].astype(ext.dtype), (1,), 'VALID',
                                   dimension_numbers=('NWC', 'WIO', 'NWC'),
                                   feature_group_count=ext.shape[-1])
    return out + b.astype(ext.dtype)


def conv_prompt(u, w, b):
    ext = jnp.pad(u, ((0, 0), (CONV_WIDTH - 1, 0), (0, 0)))
    return depthwise_conv_valid(ext, w, b), ext[:, -(CONV_WIDTH - 1):]


def make_conv_sample(buf):
    def conv(u, w, b):
        ext = jnp.concatenate([buf.astype(u.dtype), u], axis=1)
        return depthwise_conv_valid(ext, w, b), ext[:, -(CONV_WIDTH - 1):]
    return conv


def memory_kv(mem, g, w):
    B, M, _ = mem.shape
    return (rms_norm(mem, g) @ w).reshape(B, M, 2, X_HEADS, X_HEAD_DIM)


def cross_attend(q, mem_kv):
    s = jnp.einsum('bthd,bmhd->bhtm', q, mem_kv[:, :, 0].astype(q.dtype),
                   preferred_element_type=jnp.float32) * (X_HEAD_DIM ** -0.5)
    p = jax.nn.softmax(s, axis=-1)
    o = jnp.einsum('bhtm,bmhd->bthd', p, mem_kv[:, :, 1].astype(jnp.float32))
    return o.astype(q.dtype)


def token_mixing(h, pos, mem_kv, attn_fn, conv_fn, w_in, w_dw, b_dw, g_cln, b_cln,
                 w_conv_out, w_att_out, w_x_out, w_o):
    B, T, _ = h.shape
    q, k, v, a, qx, gates = jnp.split(h @ w_in, IN_SPLITS, axis=-1)
    grp = lambda t: t.reshape(B, T, N_GROUPS, HEADS_PER_GROUP, HEAD_DIM)
    q = grp(rope_partial(q.reshape(B, T, N_ATT_HEADS, HEAD_DIM), pos))
    k = grp(rope_partial(k.reshape(B, T, N_ATT_HEADS, HEAD_DIM), pos))
    v = grp(v)
    o_att, win_new = attn_fn(q, k, v)
    y_att = o_att.reshape(B, T, ATT_OUT) @ w_att_out
    u = a[..., :CONV_CH] * jax.nn.sigmoid(a[..., CONV_CH:])
    c, conv_new = conv_fn(u, w_dw, b_dw)
    y_conv = jax.nn.silu(layer_norm(c, g_cln, b_cln)) @ w_conv_out
    y_mem = cross_attend(qx.reshape(B, T, X_HEADS, X_HEAD_DIM), mem_kv).reshape(B, T, X_W) @ w_x_out
    g = jax.nn.sigmoid(gates.reshape(B, T, N_BRANCH, D_MODEL))
    merged = g[:, :, 0] * y_conv + g[:, :, 1] * y_att + g[:, :, 2] * y_mem
    return merged @ w_o, win_new, conv_new


def trunk_layer(x, pos, mem_kv, attn_fn, conv_fn, ffn1, mix, ffn2):
    x = x + 0.5 * swiglu(rms_norm(x, ffn1[0]), ffn1[1], ffn1[2], ffn1[3])
    y, win_new, conv_new = token_mixing(rms_norm(x, mix[0]), pos, mem_kv, attn_fn, conv_fn, *mix[1:])
    x = x + y
    x = x + 0.5 * swiglu(rms_norm(x, ffn2[0]), ffn2[1], ffn2[2], ffn2[3])
    return x, win_new, conv_new


def setup_inputs(seed: int = 0) -> dict:
    key = jax.random.key(seed)
    ks = iter(jax.random.split(key, 40))
    nrm = lambda shape, scale: jax.random.normal(next(ks), shape, jnp.float32) * scale
    gain = lambda shape: 1.0 + nrm(shape, 0.01)
    inp = {}
    inp['x_prompt'] = nrm((BATCH, SEQ, D_MODEL), 1.0)
    inp['x_sample'] = nrm((DEC_BATCH, DEC_SEQ, D_MODEL), 1.0)
    for gi, (win, _) in enumerate(GROUPS):
        inp['state_win%d' % gi] = nrm((DEPTH, DEC_BATCH, min(win, PAST_LEN), 2, HEADS_PER_GROUP, HEAD_DIM), 1.0)
    inp['state_conv'] = nrm((DEPTH, DEC_BATCH, CONV_WIDTH - 1, CONV_CH), 0.5)
    inp['cache_mem_kv'] = nrm((DEPTH, DEC_BATCH, N_MEM, 2, X_HEADS, X_HEAD_DIM), 1.0)
    inp['mem_prompt'] = nrm((BATCH, N_MEM, D_MODEL), 1.0)
    inp['w_ffn1_norm'] = gain((DEPTH, D_MODEL))
    inp['w_ffn1_gate'] = nrm((DEPTH, D_MODEL, D_FF), D_MODEL ** -0.5)
    inp['w_ffn1_up'] = nrm((DEPTH, D_MODEL, D_FF), D_MODEL ** -0.5)
    inp['w_ffn1_down'] = nrm((DEPTH, D_FF, D_MODEL), D_FF ** -0.5)
    inp['w_mix_norm'] = gain((DEPTH, D_MODEL))
    inp['w_in'] = nrm((DEPTH, D_MODEL, IN_W), D_MODEL ** -0.5)
    inp['w_dw'] = nrm((DEPTH, CONV_WIDTH, CONV_CH), CONV_WIDTH ** -0.5)
    inp['b_dw'] = nrm((DEPTH, CONV_CH), 0.01)
    inp['g_cln'] = gain((DEPTH, CONV_CH))
    inp['b_cln'] = nrm((DEPTH, CONV_CH), 0.01)
    inp['w_conv_out'] = nrm((DEPTH, CONV_CH, D_MODEL), CONV_CH ** -0.5)
    inp['w_att_out'] = nrm((DEPTH, ATT_OUT, D_MODEL), ATT_OUT ** -0.5)
    inp['w_x_out'] = nrm((DEPTH, X_W, D_MODEL), X_W ** -0.5)
    inp['w_o'] = nrm((DEPTH, D_MODEL, D_MODEL), D_MODEL ** -0.5)
    inp['w_mem_norm'] = gain((DEPTH, D_MODEL))
    inp['w_mem_kv'] = nrm((DEPTH, D_MODEL, 2 * X_W), D_MODEL ** -0.5)
    inp['w_ffn2_norm'] = gain((DEPTH, D_MODEL))
    inp['w_ffn2_gate'] = nrm((DEPTH, D_MODEL, D_FF), D_MODEL ** -0.5)
    inp['w_ffn2_up'] = nrm((DEPTH, D_MODEL, D_FF), D_MODEL ** -0.5)
    inp['w_ffn2_down'] = nrm((DEPTH, D_FF, D_MODEL), D_FF ** -0.5)
    inp['w_final_norm'] = gain((D_MODEL,))
    return inp


def reference(x_prompt, x_sample, state_win0, state_win1, state_win2, state_conv, cache_mem_kv, mem_prompt,
              w_ffn1_norm, w_ffn1_gate, w_ffn1_up, w_ffn1_down, w_mix_norm, w_in, w_dw, b_dw, g_cln, b_cln,
              w_conv_out, w_att_out, w_x_out, w_o, w_mem_norm, w_mem_kv,
              w_ffn2_norm, w_ffn2_gate, w_ffn2_up, w_ffn2_down, w_final_norm):
    pos_p = jnp.arange(x_prompt.shape[1])
    pos_s = PAST_LEN + jnp.arange(x_sample.shape[1])
    win_bufs = (state_win0, state_win1, state_win2)
    hp, hs = x_prompt, x_sample
    win_p = [[] for _ in GROUPS]
    win_s = [[] for _ in GROUPS]
    conv_p, conv_s, mem_p = [], [], []
    for l in range(DEPTH):
        ffn1 = (w_ffn1_norm[l], w_ffn1_gate[l], w_ffn1_up[l], w_ffn1_down[l])
        ffn2 = (w_ffn2_norm[l], w_ffn2_gate[l], w_ffn2_up[l], w_ffn2_down[l])
        mix = (w_mix_norm[l], w_in[l], w_dw[l], b_dw[l], g_cln[l], b_cln[l],
               w_conv_out[l], w_att_out[l], w_x_out[l], w_o[l])
        mkv = memory_kv(mem_prompt, w_mem_norm[l], w_mem_kv[l])
        hp, wp, cp = trunk_layer(hp, pos_p, mkv, attn_prompt, conv_prompt, ffn1, mix, ffn2)
        hs, ws, cs = trunk_layer(hs, pos_s, cache_mem_kv[l],
                                 make_attn_sample([b[l] for b in win_bufs]),
                                 make_conv_sample(state_conv[l]), ffn1, mix, ffn2)
        for gi in range(N_GROUPS):
            win_p[gi].append(wp[gi])
            win_s[gi].append(ws[gi])
        conv_p.append(cp)
        conv_s.append(cs)
        mem_p.append(mkv)
    st = lambda xs: jnp.stack(xs, axis=0)
    y_prompt = rms_norm(hp, w_final_norm)
    y_sample = rms_norm(hs, w_final_norm)
    return (y_prompt, y_sample, st(win_p[0]), st(win_p[1]), st(win_p[2]), st(conv_p), st(mem_p),
            st(win_s[0]), st(win_s[1]), st(win_s[2]), st(conv_s))
```

```python
import functools

import jax
import jax.numpy as jnp
from jax import lax
from jax.experimental import pallas as pl
from jax.experimental.pallas import tpu as pltpu

D_MODEL = 2048
BATCH = 2
SEQ = 4096
DEPTH = 2
DEC_BATCH = 32
PAST_LEN = 8192
HEAD_DIM = 128
HEADS_PER_GROUP = 4
GROUPS = ((128, 1), (512, 4), (2048, 16))
N_GROUPS = len(GROUPS)
ATT_W = N_GROUPS * HEADS_PER_GROUP * HEAD_DIM
ATT_OUT = HEADS_PER_GROUP * HEAD_DIM
BLOCK = 128
ROT_DIM = HEAD_DIM // 4
ROPE_THETA = 500000.0
CONV_CH = 3 * D_MODEL // 4
CONV_WIDTH = 31
X_HEADS = 4
X_HEAD_DIM = D_MODEL // 8
X_W = X_HEADS * X_HEAD_DIM
N_MEM = 256
D_FF = 256 * ((8 * D_MODEL // 3 + 255) // 256)
N_BRANCH = 3
QKV_W = 3 * ATT_W
COL_A = QKV_W
COL_QX = COL_A + 2 * CONV_CH
COL_GATES = COL_QX + X_W
IN_W = COL_GATES + N_BRANCH * D_MODEL
EPS = 1e-6
NEG = -1e30

F32 = jnp.float32
BF16 = jnp.bfloat16

LANES = 128
VMEM_LIMIT_BYTES = 56 * 1024 * 1024
TN = 512
TF = 256
CONV_ROWS = 128
CONV_HALO = 32


def _params(*sem):
    return pltpu.CompilerParams(dimension_semantics=sem, vmem_limit_bytes=VMEM_LIMIT_BYTES)


def _rms(x, g):
    return x * lax.rsqrt(jnp.mean(x * x, axis=-1, keepdims=True) + EPS) * g


def _dot(a, b):
    return jnp.dot(a, b, preferred_element_type=F32)


def _dot_t(a, b):
    return lax.dot_general(a, b, (((1,), (1,)), ((), ())), preferred_element_type=F32)


def _ffn_body(x_ref, g_ref, wg_ref, wu_ref, wd_ref, *rest, n_f, final_norm):
    if final_norm:
        gf_ref, o_ref, h_ref = rest
    else:
        o_ref, h_ref = rest
    f = pl.program_id(1)

    @pl.when(f == 0)
    def _():
        h_ref[...] = _rms(x_ref[...], g_ref[...]).astype(BF16)
        o_ref[...] = jnp.zeros_like(o_ref)

    h = h_ref[...]
    gate = _dot(h, wg_ref[...])
    up = _dot(h, wu_ref[...])
    act = (gate * jax.nn.sigmoid(gate) * up).astype(BF16)
    o_ref[...] += _dot(act, wd_ref[...])

    @pl.when(f == n_f - 1)
    def _():
        y = x_ref[...] + 0.5 * o_ref[...]
        if final_norm:
            y = _rms(y, gf_ref[...])
        o_ref[...] = y


def _ffn(x, g, wg, wu, wd, gf=None, *, tm):
    m = x.shape[0]
    n_f = D_FF // TF
    final_norm = gf is not None
    in_specs = [
        pl.BlockSpec((tm, D_MODEL), lambda i, f: (i, 0)),
        pl.BlockSpec((1, D_MODEL), lambda i, f: (0, 0)),
        pl.BlockSpec((D_MODEL, TF), lambda i, f: (0, f)),
        pl.BlockSpec((D_MODEL, TF), lambda i, f: (0, f)),
        pl.BlockSpec((TF, D_MODEL), lambda i, f: (f, 0)),
    ]
    args = [x, g, wg, wu, wd]
    if final_norm:
        in_specs.append(pl.BlockSpec((1, D_MODEL), lambda i, f: (0, 0)))
        args.append(gf)
    return pl.pallas_call(
        functools.partial(_ffn_body, n_f=n_f, final_norm=final_norm),
        grid=(m // tm, n_f),
        in_specs=in_specs,
        out_specs=pl.BlockSpec((tm, D_MODEL), lambda i, f: (i, 0)),
        out_shape=jax.ShapeDtypeStruct((m, D_MODEL), F32),
        scratch_shapes=[pltpu.VMEM((tm, D_MODEL), BF16)],
        compiler_params=_params("parallel", "arbitrary"),
        name="ffn",
    )(*args)


def _rope(z, c_ref, s1_ref, s2_ref):
    w = z.shape[-1]
    return z * c_ref[...] + pltpu.roll(z, w - ROT_DIM // 2, 1) * s1_ref[...] + pltpu.roll(z, ROT_DIM // 2, 1) * s2_ref[...]


def _proj_body(x_ref, g_ref, *rest, mode):
    if mode == "rope":
        w_ref, c_ref, s1_ref, s2_ref, o_ref, h_ref = rest
    elif mode == "glu":
        w_ref, w2_ref, o_ref, h_ref = rest
    else:
        w_ref, o_ref, h_ref = rest
    j = pl.program_id(1)

    @pl.when(j == 0)
    def _():
        h_ref[...] = _rms(x_ref[...], g_ref[...]).astype(BF16)

    z = _dot(h_ref[...], w_ref[...])
    if mode == "rope":
        n_rot = 2 * ATT_W // TN

        @pl.when(j < n_rot)
        def _():
            o_ref[...] = _rope(z, c_ref, s1_ref, s2_ref)

        @pl.when(j >= n_rot)
        def _():
            o_ref[...] = z
    elif mode == "glu":
        o_ref[...] = z * jax.nn.sigmoid(_dot(h_ref[...], w2_ref[...]))
    elif mode == "sigmoid":
        o_ref[...] = jax.nn.sigmoid(z)
    else:
        o_ref[...] = z


def _proj(x, g, w, *, tm, col0, ncols, mode, tables=None):
    m = x.shape[0]
    base = col0 // TN
    n_j = ncols // TN
    in_specs = [
        pl.BlockSpec((tm, D_MODEL), lambda i, j: (i, 0)),
        pl.BlockSpec((1, D_MODEL), lambda i, j: (0, 0)),
        pl.BlockSpec((D_MODEL, TN), lambda i, j: (0, base + j)),
    ]
    args = [x, g, w]
    if mode == "glu":
        in_specs.append(pl.BlockSpec((D_MODEL, TN), lambda i, j: (0, base + n_j + j)))
        args.append(w)
    if mode == "rope":
        n_tab = tables[0].shape[0] // tm
        for t in tables:
            in_specs.append(pl.BlockSpec((tm, TN), lambda i, j: (i % n_tab, 0)))
            args.append(t)
    return pl.pallas_call(
        functools.partial(_proj_body, mode=mode),
        grid=(m // tm, n_j),
        in_specs=in_specs,
        out_specs=pl.BlockSpec((tm, TN), lambda i, j: (i, j)),
        out_shape=jax.ShapeDtypeStruct((m, ncols), F32),
        scratch_shapes=[pltpu.VMEM((tm, D_MODEL), BF16)],
        compiler_params=_params("parallel", "arbitrary"),
        name="proj_" + mode,
    )(*args)


def _rope_tables(pos):
    half = ROT_DIM // 2
    inv = ROPE_THETA ** (-jnp.arange(half, dtype=F32) / half)
    ang = pos.astype(F32)[:, None] * inv[None, :]
    cos, sin = jnp.cos(ang), jnp.sin(ang)
    n = pos.shape[0]
    zeros = jnp.zeros((n, half), F32)
    rest0 = jnp.zeros((n, HEAD_DIM - ROT_DIM), F32)
    c = jnp.concatenate([cos, cos, jnp.ones((n, HEAD_DIM - ROT_DIM), F32)], axis=1)
    s1 = jnp.concatenate([-sin, zeros, rest0], axis=1)
    s2 = jnp.concatenate([zeros, sin, rest0], axis=1)
    reps = TN // HEAD_DIM
    return tuple(jnp.tile(t, (1, reps)) for t in (c, s1, s2))


def _softmax_parts(s):
    m = jnp.max(s, axis=-1, keepdims=True)
    p = jnp.exp(s - m)
    return m, p, jnp.sum(p, axis=-1, keepdims=True)


def _lane_pack(cols, rows):
    lane = lax.broadcasted_iota(jnp.int32, (rows, LANES), 1)
    out = jnp.zeros((rows, LANES), F32)
    for h, c in enumerate(cols):
        out = jnp.where(lane == h, c, out)
    return out


def _attn_prompt_body(q_ref, kp_ref, kc_ref, vp_ref, vc_ref, o_ref, lse_ref):
    n = pl.program_id(2)
    qi = lax.broadcasted_iota(jnp.int32, (BLOCK, 2 * BLOCK), 0)
    ki = lax.broadcasted_iota(jnp.int32, (BLOCK, 2 * BLOCK), 1)
    dist = qi + BLOCK - ki
    mask = (dist >= 0) & (dist <= BLOCK) & ((ki >= BLOCK) | (n > 0))
    lses = []
    for h in range(HEADS_PER_GROUP):
        hs = slice(h * HEAD_DIM, (h + 1) * HEAD_DIM)
        q = q_ref[:, hs].astype(BF16)
        k = jnp.concatenate([kp_ref[:, hs], kc_ref[:, hs]], axis=0).astype(BF16)
        v = jnp.concatenate([vp_ref[:, hs], vc_ref[:, hs]], axis=0).astype(BF16)
        s = jnp.where(mask, _dot_t(q, k) * (HEAD_DIM ** -0.5), NEG)
        m, p, den = _softmax_parts(s)
        o_ref[:, hs] = _dot(p.astype(BF16), v) / den
        lses.append(m + jnp.log(den))
    lse_ref[...] = _lane_pack(lses, BLOCK)


def _attn_prompt(qkv, gi, dil):
    sub = SEQ // dil
    nb = sub // BLOCK
    nct = QKV_W // TN
    kt, vt = ATT_W // TN + gi, 2 * ATT_W // TN + gi
    qkv3 = qkv.reshape(BATCH, sub, dil * QKV_W)
    blk = (None, BLOCK, ATT_OUT)
    o, lse = pl.pallas_call(
        _attn_prompt_body,
        grid=(BATCH, dil, nb),
        in_specs=[
            pl.BlockSpec(blk, lambda b, r, n: (b, n, r * nct + gi)),
            pl.BlockSpec(blk, lambda b, r, n: (b, jnp.maximum(n - 1, 0), r * nct + kt)),
            pl.BlockSpec(blk, lambda b, r, n: (b, n, r * nct + kt)),
            pl.BlockSpec(blk, lambda b, r, n: (b, jnp.maximum(n - 1, 0), r * nct + vt)),
            pl.BlockSpec(blk, lambda b, r, n: (b, n, r * nct + vt)),
        ],
        out_specs=[
            pl.BlockSpec(blk, lambda b, r, n: (b, n, r)),
            pl.BlockSpec((None, BLOCK, LANES), lambda b, r, n: (b, n, r)),
        ],
        out_shape=[
            jax.ShapeDtypeStruct((BATCH, sub, dil * ATT_OUT), F32),
            jax.ShapeDtypeStruct((BATCH, sub, dil * LANES), F32),
        ],
        compiler_params=_params("parallel", "parallel", "arbitrary"),
        name="attn_prompt_g%d" % gi,
    )(qkv3, qkv3, qkv3, qkv3, qkv3)
    return o.reshape(BATCH * SEQ, ATT_OUT), lse.reshape(BATCH * SEQ, LANES)


def _ln_swish(c, g_ref, b_ref):
    mu = jnp.mean(c, axis=-1, keepdims=True)
    xc = c - mu
    y = xc * lax.rsqrt(jnp.mean(xc * xc, axis=-1, keepdims=True) + EPS) * g_ref[...] + b_ref[...]
    return y * jax.nn.sigmoid(y)


def _conv_prompt_body(u_ref, halo_ref, w_ref, b_ref, g_ref, bl_ref, o_ref, tail_ref, ext_ref, c_ref, *, n_t):
    t = pl.program_id(1)
    halo = halo_ref[...]
    ext_ref[0:CONV_HALO, :] = jnp.where(t > 0, halo, jnp.zeros_like(halo))
    ext_ref[CONV_HALO:, :] = u_ref[...]
    first = CONV_HALO - (CONV_WIDTH - 1)
    for c in range(CONV_CH // LANES):
        cs = slice(c * LANES, (c + 1) * LANES)
        acc = jnp.zeros((CONV_ROWS, LANES), F32)
        for k in range(CONV_WIDTH):
            acc = acc + ext_ref[first + k:first + k + CONV_ROWS, cs] * w_ref[k:k + 1, cs]
        c_ref[:, cs] = acc
    o_ref[...] = _ln_swish(c_ref[...] + b_ref[...], g_ref, bl_ref).astype(BF16)

    @pl.when(t == n_t - 1)
    def _():
        tail_ref[...] = u_ref[CONV_ROWS - (CONV_WIDTH - 1):, :]


def _conv_prompt(u, w, b, g, bl):
    n_t = SEQ // CONV_ROWS
    hpb = CONV_ROWS // CONV_HALO
    vec = pl.BlockSpec((1, CONV_CH), lambda b_, t: (0, 0))
    return pl.pallas_call(
        functools.partial(_conv_prompt_body, n_t=n_t),
        grid=(BATCH, n_t),
        in_specs=[
            pl.BlockSpec((CONV_ROWS, CONV_CH), lambda b_, t: (b_ * n_t + t, 0)),
            pl.BlockSpec((CONV_HALO, CONV_CH), lambda b_, t: (jnp.maximum((b_ * n_t + t) * hpb - 1, 0), 0)),
            pl.BlockSpec((CONV_WIDTH, CONV_CH), lambda b_, t: (0, 0)),
            vec, vec, vec,
        ],
        out_specs=[
            pl.BlockSpec((CONV_ROWS, CONV_CH), lambda b_, t: (b_ * n_t + t, 0)),
            pl.BlockSpec((None, CONV_WIDTH - 1, CONV_CH), lambda b_, t: (b_, 0, 0)),
        ],
        out_shape=[
            jax.ShapeDtypeStruct((BATCH * SEQ, CONV_CH), BF16),
            jax.ShapeDtypeStruct((BATCH, CONV_WIDTH - 1, CONV_CH), F32),
        ],
        scratch_shapes=[
            pltpu.VMEM((CONV_HALO + CONV_ROWS, CONV_CH), F32),
            pltpu.VMEM((CONV_ROWS, CONV_CH), F32),
        ],
        compiler_params=_params("parallel", "arbitrary"),
        name="conv_prompt",
    )(u, u, w, b, g, bl)


def _cross_prompt_body(q_ref, k_ref, v_ref, o_ref):
    for h in range(TN // X_HEAD_DIM):
        hs = slice(h * X_HEAD_DIM, (h + 1) * X_HEAD_DIM)
        s = _dot_t(q_ref[:, hs].astype(BF16), k_ref[:, hs].astype(BF16)) * (X_HEAD_DIM ** -0.5)
        _, p, den = _softmax_parts(s)
        o_ref[:, hs] = _dot(p.astype(BF16), v_ref[:, hs].astype(BF16)) / den


def _cross_prompt(qx, mkv, *, tq):
    n_t = SEQ // tq
    n_hp = X_W // TN
    mkv3 = mkv.reshape(BATCH, N_MEM, 2 * X_W)
    return pl.pallas_call(
        _cross_prompt_body,
        grid=(BATCH, n_t, n_hp),
        in_specs=[
            pl.BlockSpec((tq, TN), lambda b, t, hp: (b * n_t + t, hp)),
            pl.BlockSpec((None, N_MEM, TN), lambda b, t, hp: (b, 0, hp)),
            pl.BlockSpec((None, N_MEM, TN), lambda b, t, hp: (b, 0, n_hp + hp)),
        ],
        out_specs=pl.BlockSpec((tq, TN), lambda b, t, hp: (b * n_t + t, hp)),
        out_shape=jax.ShapeDtypeStruct((BATCH * SEQ, X_W), F32),
        compiler_params=_params("parallel", "parallel", "arbitrary"),
        name="cross_prompt",
    )(qx, mkv3, mkv3)


def _post_body(c_ref, o0_ref, o1_ref, o2_ref, l0_ref, l1_ref, l2_ref, om_ref, ga_ref, gb_ref, gc_ref,
               wc_ref, wa_ref, wx_ref, wo_ref, x_ref, out_ref, oatt_ref, omem_ref, merged_ref, *, n_n):
    n = pl.program_id(1)

    @pl.when(n == 0)
    def _():
        for h in range(HEADS_PER_GROUP):
            hs = slice(h * HEAD_DIM, (h + 1) * HEAD_DIM)
            l0, l1, l2 = l0_ref[:, h:h + 1], l1_ref[:, h:h + 1], l2_ref[:, h:h + 1]
            m = jnp.maximum(jnp.maximum(l0, l1), l2)
            e0, e1, e2 = jnp.exp(l0 - m), jnp.exp(l1 - m), jnp.exp(l2 - m)
            mix = (e0 * o0_ref[:, hs] + e1 * o1_ref[:, hs] + e2 * o2_ref[:, hs]) / (e0 + e1 + e2)
            oatt_ref[:, hs] = mix.astype(BF16)
        omem_ref[...] = om_ref[...].astype(BF16)

    merged = (ga_ref[...] * _dot(c_ref[...], wc_ref[...])
              + gb_ref[...] * _dot(oatt_ref[...], wa_ref[...])
              + gc_ref[...] * _dot(omem_ref[...], wx_ref[...]))
    merged_ref[:, pl.ds(pl.multiple_of(n * TN, TN), TN)] = merged.astype(BF16)

    @pl.when(n == n_n - 1)
    def _():
        out_ref[...] = x_ref[...] + _dot(merged_ref[...], wo_ref[...])


def _post(c_act, o_g, lse_g, o_mem, sg, wc, wa, wx, wo, x, *, tm):
    m = x.shape[0]
    n_n = D_MODEL // TN
    row = lambda w: pl.BlockSpec((tm, w), lambda i, n: (i, 0))
    gate = lambda br: pl.BlockSpec((tm, TN), lambda i, n: (i, br * n_n + n))
    wcol = lambda k: pl.BlockSpec((k, TN), lambda i, n: (0, n))
    return pl.pallas_call(
        functools.partial(_post_body, n_n=n_n),
        grid=(m // tm, n_n),
        in_specs=[
            row(CONV_CH), row(ATT_OUT), row(ATT_OUT), row(ATT_OUT), row(LANES), row(LANES), row(LANES), row(X_W),
            gate(0), gate(1), gate(2),
            wcol(CONV_CH), wcol(ATT_OUT), wcol(X_W),
            pl.BlockSpec((D_MODEL, D_MODEL), lambda i, n: (0, 0), pipeline_mode=pl.Buffered(1)),
            row(D_MODEL),
        ],
        out_specs=row(D_MODEL),
        out_shape=jax.ShapeDtypeStruct((m, D_MODEL), F32),
        scratch_shapes=[
            pltpu.VMEM((tm, ATT_OUT), BF16),
            pltpu.VMEM((tm, X_W), BF16),
            pltpu.VMEM((tm, D_MODEL), BF16),
        ],
        compiler_params=_params("parallel", "arbitrary"),
        name="post",
    )(c_act, *o_g, *lse_g, o_mem, sg, sg, sg, wc, wa, wx, wo, x)


def _mix_sample_body(qkv_ref, qx_ref, u_ref, w0_ref, w1_ref, w2_ref, cs_ref, mem_ref, wdw_ref, b_ref, g_ref, bl_ref,
                     o0_ref, o1_ref, o2_ref, l0_ref, l1_ref, l2_ref, c_ref, om_ref):
    for gi, (win_ref, o_ref, l_ref) in enumerate(((w0_ref, o0_ref, l0_ref), (w1_ref, o1_ref, l1_ref),
                                                  (w2_ref, o2_ref, l2_ref))):
        lses = []
        for h in range(HEADS_PER_GROUP):
            col = gi * ATT_OUT + h * HEAD_DIM
            q = qkv_ref[:, col:col + HEAD_DIM]
            k_new = qkv_ref[:, ATT_W + col:ATT_W + col + HEAD_DIM]
            v_new = qkv_ref[:, 2 * ATT_W + col:2 * ATT_W + col + HEAD_DIM]
            k_buf = win_ref[:, h * HEAD_DIM:(h + 1) * HEAD_DIM]
            v_buf = win_ref[:, ATT_OUT + h * HEAD_DIM:ATT_OUT + (h + 1) * HEAD_DIM]
            scale = HEAD_DIM ** -0.5
            s_buf = jnp.sum(k_buf * q, axis=-1, keepdims=True) * scale
            s_new = jnp.sum(k_new * q, axis=-1, keepdims=True) * scale
            m = jnp.maximum(jnp.max(s_buf, axis=0, keepdims=True), s_new)
            p_buf, p_new = jnp.exp(s_buf - m), jnp.exp(s_new - m)
            den = jnp.sum(p_buf, axis=0, keepdims=True) + p_new
            o = (jnp.sum(p_buf * v_buf, axis=0, keepdims=True) + p_new * v_new) / den
            o_ref[:, h * HEAD_DIM:(h + 1) * HEAD_DIM] = o
            lses.append(m + jnp.log(den))
        l_ref[...] = _lane_pack(lses, 1)
    for h in range(X_HEADS):
        hs = slice(h * X_HEAD_DIM, (h + 1) * X_HEAD_DIM)
        s = jnp.sum(mem_ref[:, hs] * qx_ref[:, hs], axis=-1, keepdims=True) * (X_HEAD_DIM ** -0.5)
        p = jnp.exp(s - jnp.max(s, axis=0, keepdims=True))
        v = mem_ref[:, X_W + h * X_HEAD_DIM:X_W + (h + 1) * X_HEAD_DIM]
        om_ref[:, hs] = jnp.sum(p * v, axis=0, keepdims=True) / jnp.sum(p, axis=0, keepdims=True)
    hist = CONV_WIDTH - 1
    c = (jnp.sum(cs_ref[...] * wdw_ref[0:hist, :], axis=0, keepdims=True)
         + u_ref[...] * wdw_ref[hist:hist + 1, :] + b_ref[...])
    c_ref[...] = _ln_swish(c, g_ref, bl_ref)


def _mix_sample(layer, qkv, qx, u, wins, conv_state, mem_kv, w, b, g, bl):
    row = lambda wd: pl.BlockSpec((None, 1, wd), lambda s: (s, 0, 0))
    vec = pl.BlockSpec((1, CONV_CH), lambda s: (0, 0))
    kv_w = 2 * ATT_OUT
    win_specs, win_args = [], []
    for (win, dil), arr in zip(GROUPS, wins):
        win_args.append(arr.reshape(DEPTH, DEC_BATCH, win // dil, dil * kv_w))
        win_specs.append(pl.BlockSpec((None, None, win // dil, kv_w), lambda s: (layer, s, 0, 0)))
    f32_row = lambda wd: jax.ShapeDtypeStruct((DEC_BATCH, 1, wd), F32)
    outs = pl.pallas_call(
        _mix_sample_body,
        grid=(DEC_BATCH,),
        in_specs=[row(QKV_W), row(X_W), row(CONV_CH), *win_specs,
                  pl.BlockSpec((None, None, CONV_WIDTH - 1, CONV_CH), lambda s: (layer, s, 0, 0)),
                  pl.BlockSpec((None, None, N_MEM, 2 * X_W), lambda s: (layer, s, 0, 0)),
                  pl.BlockSpec((CONV_WIDTH, CONV_CH), lambda s: (0, 0)), vec, vec, vec],
        out_specs=[row(ATT_OUT)] * 3 + [row(LANES)] * 3 + [row(CONV_CH), row(X_W)],
        out_shape=[f32_row(ATT_OUT)] * 3 + [f32_row(LANES)] * 3 + [f32_row(CONV_CH), f32_row(X_W)],
        compiler_params=_params("parallel"),
        name="mix_sample",
    )(qkv.reshape(DEC_BATCH, 1, QKV_W), qx.reshape(DEC_BATCH, 1, X_W), u.reshape(DEC_BATCH, 1, CONV_CH),
      *win_args, conv_state, mem_kv.reshape(DEPTH, DEC_BATCH, N_MEM, 2 * X_W), w, b, g, bl)
    flat = [o.reshape(DEC_BATCH, -1) for o in outs]
    return flat[0:3], flat[3:6], flat[6].astype(BF16), flat[7]


def kernel(x_prompt, x_sample, state_win0, state_win1, state_win2, state_conv, cache_mem_kv, mem_prompt, w_ffn1_norm, w_ffn1_gate, w_ffn1_up, w_ffn1_down, w_mix_norm, w_in, w_dw, b_dw, g_cln, b_cln, w_conv_out, w_att_out, w_x_out, w_o, w_mem_norm, w_mem_kv, w_ffn2_norm, w_ffn2_gate, w_ffn2_up, w_ffn2_down, w_final_norm):
    tm_p, tm_s = 1024, DEC_BATCH
    state_wins = (state_win0, state_win1, state_win2)
    hp = x_prompt.reshape(BATCH * SEQ, D_MODEL)
    hs = x_sample.reshape(DEC_BATCH, D_MODEL)
    mem = mem_prompt.reshape(BATCH * N_MEM, D_MODEL)
    tab_p = _rope_tables(jnp.arange(SEQ))
    tab_s = _rope_tables(jnp.full((DEC_BATCH,), PAST_LEN))
    vec = lambda a: a.reshape(1, -1)
    g_final = vec(w_final_norm)

    win_p = [[] for _ in GROUPS]
    kv_s = [[] for _ in GROUPS]
    conv_p, u_s, mem_p = [], [], []
    for l in range(DEPTH):
        last = l == DEPTH - 1
        bf = lambda a: a[l].astype(BF16)
        ffn1 = (vec(w_ffn1_norm[l]), bf(w_ffn1_gate), bf(w_ffn1_up), bf(w_ffn1_down))
        ffn2 = (vec(w_ffn2_norm[l]), bf(w_ffn2_gate), bf(w_ffn2_up), bf(w_ffn2_down))
        g_mix, win_l = vec(w_mix_norm[l]), bf(w_in)
        conv_w = (w_dw[l], vec(b_dw[l]), vec(g_cln[l]), vec(b_cln[l]))
        out_w = (bf(w_conv_out), bf(w_att_out), bf(w_x_out), bf(w_o))

        mkv = _proj(mem, vec(w_mem_norm[l]), bf(w_mem_kv), tm=BATCH * N_MEM, col0=0, ncols=2 * X_W, mode="plain")
        mem_p.append(mkv.reshape(BATCH, N_MEM, 2, X_HEADS, X_HEAD_DIM))

        def in_proj(x, tm, tables):
            qkv = _proj(x, g_mix, win_l, tm=tm, col0=0, ncols=QKV_W, mode="rope", tables=tables)
            u = _proj(x, g_mix, win_l, tm=tm, col0=COL_A, ncols=CONV_CH, mode="glu")
            qx = _proj(x, g_mix, win_l, tm=tm, col0=COL_QX, ncols=X_W, mode="plain")
            sg = _proj(x, g_mix, win_l, tm=tm, col0=COL_GATES, ncols=N_BRANCH * D_MODEL, mode="sigmoid")
            return qkv, u, qx, sg

        hp = _ffn(hp, *ffn1, tm=tm_p)
        qkv, u, qx, sg = in_proj(hp, tm_p, tab_p)
        att = [_attn_prompt(qkv, gi, dil) for gi, (_, dil) in enumerate(GROUPS)]
        c_act, conv_tail = _conv_prompt(u, *conv_w)
        o_mem = _cross_prompt(qx, mkv, tq=1024)
        hp = _post(c_act, [a[0] for a in att], [a[1] for a in att], o_mem, sg, *out_w, hp, tm=256)
        hp = _ffn(hp, *ffn2, g_final if last else None, tm=tm_p)
        qkv3 = qkv.reshape(BATCH, SEQ, QKV_W)
        for gi, (win, _) in enumerate(GROUPS):
            keep = min(win, SEQ)
            cut = lambda c0: qkv3[:, SEQ - keep:, c0 + gi * ATT_OUT:c0 + (gi + 1) * ATT_OUT].reshape(
                BATCH, keep, HEADS_PER_GROUP, HEAD_DIM)
            win_p[gi].append(jnp.stack([cut(ATT_W), cut(2 * ATT_W)], axis=2))
        conv_p.append(conv_tail)

        hs = _ffn(hs, *ffn1, tm=tm_s)
        qkv, u, qx, sg = in_proj(hs, tm_s, tab_s)
        o_g, lse_g, c_act, o_mem = _mix_sample(l, qkv, qx, u, state_wins, state_conv, cache_mem_kv, *conv_w)
        hs = _post(c_act, o_g, lse_g, o_mem, sg, *out_w, hs, tm=tm_s)
        hs = _ffn(hs, *ffn2, g_final if last else None, tm=tm_s)
        for gi in range(N_GROUPS):
            cut = lambda c0: qkv[:, c0 + gi * ATT_OUT:c0 + (gi + 1) * ATT_OUT].reshape(
                DEC_BATCH, HEADS_PER_GROUP, HEAD_DIM)
            kv_s[gi].append(jnp.stack([cut(ATT_W), cut(2 * ATT_W)], axis=1))
        u_s.append(u)

    st = lambda xs: jnp.stack(xs, axis=0)
    shift_in = lambda state, new: jnp.concatenate([state[:, :, 1:], st(new)[:, :, None]], axis=2)
    return (hp.reshape(BATCH, SEQ, D_MODEL), hs.reshape(DEC_BATCH, 1, D_MODEL),
            st(win_p[0]), st(win_p[1]), st(win_p[2]), st(conv_p), st(mem_p),
            shift_in(state_win0, kv_s[0]), shift_in(state_win1, kv_s[1]), shift_in(state_win2, kv_s[2]),
            shift_in(state_conv, u_s))
```

```python
import functools

import jax
import jax.numpy as jnp
from jax import lax
from jax.experimental import pallas as pl
from jax.experimental.pallas import tpu as pltpu

D_MODEL = 2048
BATCH = 2
SEQ = 4096
DEPTH = 2
DEC_BATCH = 32
PAST_LEN = 8192
HEAD_DIM = 128
HEADS_PER_GROUP = 4
GROUPS = ((128, 1), (512, 4), (2048, 16))
N_GROUPS = len(GROUPS)
ATT_W = N_GROUPS * HEADS_PER_GROUP * HEAD_DIM
ATT_OUT = HEADS_PER_GROUP * HEAD_DIM
BLOCK = 128
ROT_DIM = HEAD_DIM // 4
ROPE_THETA = 500000.0
CONV_CH = 3 * D_MODEL // 4
CONV_WIDTH = 31
X_HEADS = 4
X_HEAD_DIM = D_MODEL // 8
X_W = X_HEADS * X_HEAD_DIM
N_MEM = 256
D_FF = 256 * ((8 * D_MODEL // 3 + 255) // 256)
N_BRANCH = 3
QKV_W = 3 * ATT_W
COL_A = QKV_W
COL_QX = COL_A + 2 * CONV_CH
COL_GATES = COL_QX + X_W
IN_W = COL_GATES + N_BRANCH * D_MODEL
EPS = 1e-6
NEG = -1e30

F32 = jnp.float32
BF16 = jnp.bfloat16

LANES = 128
VMEM_LIMIT_BYTES = 58 * 1024 * 1024
TN = 512
TF = 256
TM = 1024
CONV_ROWS = 128
CONV_HALO = 32


def _params(*sem):
    return pltpu.CompilerParams(dimension_semantics=sem, vmem_limit_bytes=VMEM_LIMIT_BYTES)


def _rms(x, g):
    return x * lax.rsqrt(jnp.mean(x * x, axis=-1, keepdims=True) + EPS) * g


def _dot(a, b):
    return jnp.dot(a, b, preferred_element_type=F32)


def _dot_t(a, b):
    return lax.dot_general(a, b, (((1,), (1,)), ((), ())), preferred_element_type=F32)


def _ffn_body(xp_ref, xs_ref, g_ref, wg_ref, wu_ref, wd_ref, *rest, n_f, final_norm):
    if final_norm:
        gf_ref, op_ref, os_ref, hp_ref, hs_ref = rest
    else:
        op_ref, os_ref, hp_ref, hs_ref = rest
    i, f = pl.program_id(0), pl.program_id(1)
    wg, wu, wd = wg_ref[...].astype(BF16), wu_ref[...].astype(BF16), wd_ref[...].astype(BF16)

    def rows(x_ref, o_ref, h_ref):
        @pl.when(f == 0)
        def _():
            h_ref[...] = _rms(x_ref[...], g_ref[...]).astype(BF16)
            o_ref[...] = jnp.zeros_like(o_ref)

        h = h_ref[...]
        gate = _dot(h, wg)
        act = (gate * jax.nn.sigmoid(gate) * _dot(h, wu)).astype(BF16)
        o_ref[...] += _dot(act, wd)

        @pl.when(f == n_f - 1)
        def _():
            y = x_ref[...] + 0.5 * o_ref[...]
            if final_norm:
                y = _rms(y, gf_ref[...])
            o_ref[...] = y

    rows(xp_ref, op_ref, hp_ref)

    @pl.when(i == 0)
    def _():
        rows(xs_ref, os_ref, hs_ref)


def _ffn(layer, xp, xs, g, wg, wu, wd, gf=None):
    m, ms = xp.shape[0], xs.shape[0]
    n_f = D_FF // TF
    final_norm = gf is not None
    vec = pl.BlockSpec((1, D_MODEL), lambda i, f: (0, 0))
    in_specs = [
        pl.BlockSpec((TM, D_MODEL), lambda i, f: (i, 0), pipeline_mode=pl.Buffered(1)),
        pl.BlockSpec((ms, D_MODEL), lambda i, f: (0, 0)),
        vec,
        pl.BlockSpec((None, D_MODEL, TF), lambda i, f: (layer, 0, f)),
        pl.BlockSpec((None, D_MODEL, TF), lambda i, f: (layer, 0, f)),
        pl.BlockSpec((None, TF, D_MODEL), lambda i, f: (layer, f, 0)),
    ]
    args = [xp, xs, g, wg, wu, wd]
    if final_norm:
        in_specs.append(vec)
        args.append(gf)
    return pl.pallas_call(
        functools.partial(_ffn_body, n_f=n_f, final_norm=final_norm),
        grid=(m // TM, n_f),
        in_specs=in_specs,
        out_specs=[pl.BlockSpec((TM, D_MODEL), lambda i, f: (i, 0)),
                   pl.BlockSpec((ms, D_MODEL), lambda i, f: (0, 0))],
        out_shape=[jax.ShapeDtypeStruct((m, D_MODEL), F32), jax.ShapeDtypeStruct((ms, D_MODEL), F32)],
        scratch_shapes=[pltpu.VMEM((TM, D_MODEL), BF16), pltpu.VMEM((ms, D_MODEL), BF16)],
        compiler_params=_params("arbitrary", "arbitrary"),
        name="ffn",
    )(*args)


def _rope(z, c_ref, s1_ref, s2_ref):
    half = ROT_DIM // 2
    c, s1, s2 = c_ref[...], s1_ref[...], s2_ref[...]
    heads = []
    for h in range(z.shape[-1] // HEAD_DIM):
        zh = z[:, h * HEAD_DIM:(h + 1) * HEAD_DIM]
        heads.append(zh * c + pltpu.roll(zh, HEAD_DIM - half, 1) * s1 + pltpu.roll(zh, half, 1) * s2)
    return jnp.concatenate(heads, axis=1)


def _col_tile(ref, t):
    return ref.at[:, pl.ds(pl.multiple_of(t * TN, TN), TN)]


def _proj_qkvu_body(xp_ref, xs_ref, g_ref, w_ref, w2_ref, cp_ref, s1p_ref, s2p_ref, cs_ref, s1s_ref, s2s_ref,
                    qkvp_ref, up_ref, qkvs_ref, us_ref, hp_ref, hs_ref):
    i, s = pl.program_id(0), pl.program_id(1)
    n_rot, n_qkv = 2 * ATT_W // TN, QKV_W // TN
    w = w_ref[...].astype(BF16)

    def rows(x_ref, h_ref, tabs, qkv_ref, u_ref):
        @pl.when(s == 0)
        def _():
            h_ref[...] = _rms(x_ref[...], g_ref[...]).astype(BF16)

        h = h_ref[...]
        z = _dot(h, w)

        @pl.when(s < n_rot)
        def _():
            qkv_ref[...] = _rope(z, *tabs)

        @pl.when((s >= n_rot) & (s < n_qkv))
        def _():
            qkv_ref[...] = z

        @pl.when(s >= n_qkv)
        def _():
            u_ref[...] = z * jax.nn.sigmoid(_dot(h, w2_ref[...].astype(BF16)))

    rows(xp_ref, hp_ref, (cp_ref, s1p_ref, s2p_ref), qkvp_ref, up_ref)

    @pl.when(i == 0)
    def _():
        rows(xs_ref, hs_ref, (cs_ref, s1s_ref, s2s_ref),
             _col_tile(qkvs_ref, jnp.minimum(s, n_qkv - 1)), _col_tile(us_ref, jnp.maximum(s - n_qkv, 0)))


def _proj_qxg_body(xp_ref, xs_ref, g_ref, w_ref, qxp_ref, sgp_ref, qxs_ref, sgs_ref, hp_ref, hs_ref):
    i, s = pl.program_id(0), pl.program_id(1)
    n_qx = X_W // TN
    w = w_ref[...].astype(BF16)

    def rows(x_ref, h_ref, qx_ref, sg_ref):
        @pl.when(s == 0)
        def _():
            h_ref[...] = _rms(x_ref[...], g_ref[...]).astype(BF16)

        z = _dot(h_ref[...], w)

        @pl.when(s < n_qx)
        def _():
            qx_ref[...] = z

        @pl.when(s >= n_qx)
        def _():
            sg_ref[...] = jax.nn.sigmoid(z)

    rows(xp_ref, hp_ref, qxp_ref, sgp_ref)

    @pl.when(i == 0)
    def _():
        rows(xs_ref, hs_ref, _col_tile(qxs_ref, jnp.minimum(s, n_qx - 1)), _col_tile(sgs_ref, jnp.maximum(s - n_qx, 0)))


def _in_proj(layer, xp, xs, g, w_in, tab_p, tab_s):
    m, ms = xp.shape[0], xs.shape[0]
    n_tab = tab_p[0].shape[0] // TM
    n_qkv, n_u, n_qx, n_g = QKV_W // TN, CONV_CH // TN, X_W // TN, N_BRANCH * D_MODEL // TN
    a0, qx0 = COL_A // TN, COL_QX // TN
    xspec = [pl.BlockSpec((TM, D_MODEL), lambda i, s: (i, 0), pipeline_mode=pl.Buffered(1)),
             pl.BlockSpec((ms, D_MODEL), lambda i, s: (0, 0)),
             pl.BlockSpec((1, D_MODEL), lambda i, s: (0, 0))]
    wspec = lambda fn: pl.BlockSpec((None, D_MODEL, TN), lambda i, s: (layer, 0, fn(s)))
    ptile = lambda fn: pl.BlockSpec((TM, TN), lambda i, s: (i, fn(s)))
    whole = lambda wd: pl.BlockSpec((ms, wd), lambda i, s: (0, 0))
    shp = lambda r, wd: jax.ShapeDtypeStruct((r, wd), F32)
    scratch = [pltpu.VMEM((TM, D_MODEL), BF16), pltpu.VMEM((ms, D_MODEL), BF16)]

    qkv_p, u_p, qkv_s, u_s = pl.pallas_call(
        _proj_qkvu_body,
        grid=(m // TM, n_qkv + n_u),
        in_specs=xspec + [
            wspec(lambda s: s),
            wspec(lambda s: jnp.clip(s + n_u, a0 + n_u, a0 + 2 * n_u - 1)),
            *[pl.BlockSpec((TM, HEAD_DIM), lambda i, s: (i % n_tab, 0)) for _ in range(3)],
            *[pl.BlockSpec((ms, HEAD_DIM), lambda i, s: (0, 0)) for _ in range(3)],
        ],
        out_specs=[ptile(lambda s: jnp.minimum(s, n_qkv - 1)), ptile(lambda s: jnp.maximum(s - n_qkv, 0)),
                   whole(QKV_W), whole(CONV_CH)],
        out_shape=[shp(m, QKV_W), shp(m, CONV_CH), shp(ms, QKV_W), shp(ms, CONV_CH)],
        scratch_shapes=scratch,
        compiler_params=_params("arbitrary", "arbitrary"),
        name="proj_qkvu",
    )(xp, xs, g, w_in, w_in, *tab_p, *tab_s)

    qx_p, sg_p, qx_s, sg_s = pl.pallas_call(
        _proj_qxg_body,
        grid=(m // TM, n_qx + n_g),
        in_specs=xspec + [wspec(lambda s: qx0 + s)],
        out_specs=[ptile(lambda s: jnp.minimum(s, n_qx - 1)), ptile(lambda s: jnp.maximum(s - n_qx, 0)),
                   whole(X_W), whole(N_BRANCH * D_MODEL)],
        out_shape=[shp(m, X_W), shp(m, N_BRANCH * D_MODEL), shp(ms, X_W), shp(ms, N_BRANCH * D_MODEL)],
        scratch_shapes=scratch,
        compiler_params=_params("arbitrary", "arbitrary"),
        name="proj_qxg",
    )(xp, xs, g, w_in)
    return (qkv_p, u_p, qx_p, sg_p), (qkv_s, u_s, qx_s, sg_s)


def _mem_proj_body(x_ref, g_ref, w_ref, o_ref, h_ref):
    @pl.when(pl.program_id(0) == 0)
    def _():
        h_ref[...] = _rms(x_ref[...], g_ref[...]).astype(BF16)

    o_ref[...] = _dot(h_ref[...], w_ref[...].astype(BF16))


def _mem_proj(layer, x, g, w):
    m, n = x.shape[0], w.shape[-1]
    return pl.pallas_call(
        _mem_proj_body,
        grid=(n // TN,),
        in_specs=[pl.BlockSpec((m, D_MODEL), lambda j: (0, 0)),
                  pl.BlockSpec((1, D_MODEL), lambda j: (0, 0)),
                  pl.BlockSpec((None, D_MODEL, TN), lambda j: (layer, 0, j))],
        out_specs=pl.BlockSpec((m, TN), lambda j: (0, j)),
        out_shape=jax.ShapeDtypeStruct((m, n), F32),
        scratch_shapes=[pltpu.VMEM((m, D_MODEL), BF16)],
        compiler_params=_params("arbitrary"),
        name="mem_proj",
    )(x, g, w)


def _rope_tables(pos):
    half = ROT_DIM // 2
    inv = ROPE_THETA ** (-jnp.arange(half, dtype=F32) / half)
    ang = pos.astype(F32)[:, None] * inv[None, :]
    cos, sin = jnp.cos(ang), jnp.sin(ang)
    n = pos.shape[0]
    zeros = jnp.zeros((n, half), F32)
    rest0 = jnp.zeros((n, HEAD_DIM - ROT_DIM), F32)
    c = jnp.concatenate([cos, cos, jnp.ones((n, HEAD_DIM - ROT_DIM), F32)], axis=1)
    s1 = jnp.concatenate([-sin, zeros, rest0], axis=1)
    s2 = jnp.concatenate([zeros, sin, rest0], axis=1)
    return c, s1, s2


def _softmax_parts(s):
    m = jnp.max(s, axis=-1, keepdims=True)
    p = jnp.exp(s - m)
    return m, p, jnp.sum(p, axis=-1, keepdims=True)


def _lane_pack(cols, rows):
    lane = lax.broadcasted_iota(jnp.int32, (rows, LANES), 1)
    out = jnp.zeros((rows, LANES), F32)
    for h, c in enumerate(cols):
        out = jnp.where(lane == h, c, out)
    return out


def _attn_prompt_body(q_ref, kp_ref, kc_ref, vp_ref, vc_ref, o_ref, lse_ref):
    n = pl.program_id(2)
    qi = lax.broadcasted_iota(jnp.int32, (BLOCK, 2 * BLOCK), 0)
    ki = lax.broadcasted_iota(jnp.int32, (BLOCK, 2 * BLOCK), 1)
    dist = qi + BLOCK - ki
    mask = (dist >= 0) & (dist <= BLOCK) & ((ki >= BLOCK) | (n > 0))
    lses = []
    for h in range(HEADS_PER_GROUP):
        hs = slice(h * HEAD_DIM, (h + 1) * HEAD_DIM)
        q = q_ref[:, hs].astype(BF16)
        k = jnp.concatenate([kp_ref[:, hs], kc_ref[:, hs]], axis=0).astype(BF16)
        v = jnp.concatenate([vp_ref[:, hs], vc_ref[:, hs]], axis=0).astype(BF16)
        s = jnp.where(mask, _dot_t(q, k) * (HEAD_DIM ** -0.5), NEG)
        m, p, den = _softmax_parts(s)
        o_ref[:, hs] = _dot(p.astype(BF16), v) / den
        lses.append(m + jnp.log(den))
    lse_ref[...] = _lane_pack(lses, BLOCK)


def _attn_prompt(qkv, gi, dil):
    sub = SEQ // dil
    nb = sub // BLOCK
    nct = QKV_W // TN
    kt, vt = ATT_W // TN + gi, 2 * ATT_W // TN + gi
    qkv3 = qkv.reshape(BATCH, sub, dil * QKV_W)
    blk = (None, BLOCK, ATT_OUT)
    o, lse = pl.pallas_call(
        _attn_prompt_body,
        grid=(BATCH, dil, nb),
        in_specs=[
            pl.BlockSpec(blk, lambda b, r, n: (b, n, r * nct + gi)),
            pl.BlockSpec(blk, lambda b, r, n: (b, jnp.maximum(n - 1, 0), r * nct + kt)),
            pl.BlockSpec(blk, lambda b, r, n: (b, n, r * nct + kt)),
            pl.BlockSpec(blk, lambda b, r, n: (b, jnp.maximum(n - 1, 0), r * nct + vt)),
            pl.BlockSpec(blk, lambda b, r, n: (b, n, r * nct + vt)),
        ],
        out_specs=[
            pl.BlockSpec(blk, lambda b, r, n: (b, n, r)),
            pl.BlockSpec((None, BLOCK, LANES), lambda b, r, n: (b, n, r)),
        ],
        out_shape=[
            jax.ShapeDtypeStruct((BATCH, sub, dil * ATT_OUT), F32),
            jax.ShapeDtypeStruct((BATCH, sub, dil * LANES), F32),
        ],
        compiler_params=_params("parallel", "parallel", "arbitrary"),
        name="attn_prompt_g%d" % gi,
    )(qkv3, qkv3, qkv3, qkv3, qkv3)
    return o.reshape(BATCH * SEQ, ATT_OUT), lse.reshape(BATCH * SEQ, LANES)


def _ln_swish(c, g_ref, b_ref):
    mu = jnp.mean(c, axis=-1, keepdims=True)
    xc = c - mu
    y = xc * lax.rsqrt(jnp.mean(xc * xc, axis=-1, keepdims=True) + EPS) * g_ref[...] + b_ref[...]
    return y * jax.nn.sigmoid(y)


def _conv_prompt_body(u_ref, halo_ref, w_ref, b_ref, g_ref, bl_ref, o_ref, tail_ref, ext_ref, c_ref, *, n_t):
    t = pl.program_id(1)
    halo = halo_ref[...]
    ext_ref[0:CONV_HALO, :] = jnp.where(t > 0, halo, jnp.zeros_like(halo))
    ext_ref[CONV_HALO:, :] = u_ref[...]
    first = CONV_HALO - (CONV_WIDTH - 1)
    for c in range(CONV_CH // LANES):
        cs = slice(c * LANES, (c + 1) * LANES)
        acc = jnp.zeros((CONV_ROWS, LANES), F32)
        for k in range(CONV_WIDTH):
            acc = acc + ext_ref[first + k:first + k + CONV_ROWS, cs] * w_ref[k:k + 1, cs]
        c_ref[:, cs] = acc
    o_ref[...] = _ln_swish(c_ref[...] + b_ref[...], g_ref, bl_ref).astype(BF16)

    @pl.when(t == n_t - 1)
    def _():
        tail_ref[...] = u_ref[CONV_ROWS - (CONV_WIDTH - 1):, :]


def _conv_prompt(u, w, b, g, bl):
    n_t = SEQ // CONV_ROWS
    hpb = CONV_ROWS // CONV_HALO
    vec = pl.BlockSpec((1, CONV_CH), lambda b_, t: (0, 0))
    return pl.pallas_call(
        functools.partial(_conv_prompt_body, n_t=n_t),
        grid=(BATCH, n_t),
        in_specs=[
            pl.BlockSpec((CONV_ROWS, CONV_CH), lambda b_, t: (b_ * n_t + t, 0)),
            pl.BlockSpec((CONV_HALO, CONV_CH), lambda b_, t: (jnp.maximum((b_ * n_t + t) * hpb - 1, 0), 0)),
            pl.BlockSpec((CONV_WIDTH, CONV_CH), lambda b_, t: (0, 0)),
            vec, vec, vec,
        ],
        out_specs=[
            pl.BlockSpec((CONV_ROWS, CONV_CH), lambda b_, t: (b_ * n_t + t, 0)),
            pl.BlockSpec((None, CONV_WIDTH - 1, CONV_CH), lambda b_, t: (b_, 0, 0)),
        ],
        out_shape=[
            jax.ShapeDtypeStruct((BATCH * SEQ, CONV_CH), BF16),
            jax.ShapeDtypeStruct((BATCH, CONV_WIDTH - 1, CONV_CH), F32),
        ],
        scratch_shapes=[
            pltpu.VMEM((CONV_HALO + CONV_ROWS, CONV_CH), F32),
            pltpu.VMEM((CONV_ROWS, CONV_CH), F32),
        ],
        compiler_params=_params("parallel", "arbitrary"),
        name="conv_prompt",
    )(u, u, w, b, g, bl)


def _cross_prompt_body(q_ref, k_ref, v_ref, o_ref):
    for h in range(TN // X_HEAD_DIM):
        hs = slice(h * X_HEAD_DIM, (h + 1) * X_HEAD_DIM)
        s = _dot_t(q_ref[:, hs].astype(BF16), k_ref[:, hs].astype(BF16)) * (X_HEAD_DIM ** -0.5)
        _, p, den = _softmax_parts(s)
        o_ref[:, hs] = _dot(p.astype(BF16), v_ref[:, hs].astype(BF16)) / den


def _cross_prompt(qx, mkv, *, tq):
    n_t = SEQ // tq
    n_hp = X_W // TN
    mkv3 = mkv.reshape(BATCH, N_MEM, 2 * X_W)
    return pl.pallas_call(
        _cross_prompt_body,
        grid=(BATCH, n_t, n_hp),
        in_specs=[
            pl.BlockSpec((tq, TN), lambda b, t, hp: (b * n_t + t, hp)),
            pl.BlockSpec((None, N_MEM, TN), lambda b, t, hp: (b, 0, hp)),
            pl.BlockSpec((None, N_MEM, TN), lambda b, t, hp: (b, 0, n_hp + hp)),
        ],
        out_specs=pl.BlockSpec((tq, TN), lambda b, t, hp: (b * n_t + t, hp)),
        out_shape=jax.ShapeDtypeStruct((BATCH * SEQ, X_W), F32),
        compiler_params=_params("parallel", "parallel", "arbitrary"),
        name="cross_prompt",
    )(qx, mkv3, mkv3)


def _post_body(c_ref, o0_ref, o1_ref, o2_ref, l0_ref, l1_ref, l2_ref, om_ref, ga_ref, gb_ref, gc_ref,
               wc_ref, wa_ref, wx_ref, wo_ref, x_ref, out_ref, oatt_ref, omem_ref, merged_ref, *, n_n):
    n = pl.program_id(1)

    @pl.when(n == 0)
    def _():
        for h in range(HEADS_PER_GROUP):
            hs = slice(h * HEAD_DIM, (h + 1) * HEAD_DIM)
            l0, l1, l2 = l0_ref[:, h:h + 1], l1_ref[:, h:h + 1], l2_ref[:, h:h + 1]
            m = jnp.maximum(jnp.maximum(l0, l1), l2)
            e0, e1, e2 = jnp.exp(l0 - m), jnp.exp(l1 - m), jnp.exp(l2 - m)
            mix = (e0 * o0_ref[:, hs] + e1 * o1_ref[:, hs] + e2 * o2_ref[:, hs]) / (e0 + e1 + e2)
            oatt_ref[:, hs] = mix.astype(BF16)
        omem_ref[...] = om_ref[...].astype(BF16)

    merged = (ga_ref[...] * _dot(c_ref[...], wc_ref[...])
              + gb_ref[...] * _dot(oatt_ref[...], wa_ref[...])
              + gc_ref[...] * _dot(omem_ref[...], wx_ref[...]))
    merged_ref[:, pl.ds(pl.multiple_of(n * TN, TN), TN)] = merged.astype(BF16)

    @pl.when(n == n_n - 1)
    def _():
        out_ref[...] = x_ref[...] + _dot(merged_ref[...], wo_ref[...])


def _post(c_act, o_g, lse_g, o_mem, sg, wc, wa, wx, wo, x, *, tm):
    m = x.shape[0]
    n_n = D_MODEL // TN
    row = lambda w: pl.BlockSpec((tm, w), lambda i, n: (i, 0))
    gate = lambda br: pl.BlockSpec((tm, TN), lambda i, n: (i, br * n_n + n))
    wcol = lambda k: pl.BlockSpec((k, TN), lambda i, n: (0, n))
    return pl.pallas_call(
        functools.partial(_post_body, n_n=n_n),
        grid=(m // tm, n_n),
        in_specs=[
            row(CONV_CH), row(ATT_OUT), row(ATT_OUT), row(ATT_OUT), row(LANES), row(LANES), row(LANES), row(X_W),
            gate(0), gate(1), gate(2),
            wcol(CONV_CH), wcol(ATT_OUT), wcol(X_W),
            pl.BlockSpec((D_MODEL, D_MODEL), lambda i, n: (0, 0), pipeline_mode=pl.Buffered(1)),
            row(D_MODEL),
        ],
        out_specs=row(D_MODEL),
        out_shape=jax.ShapeDtypeStruct((m, D_MODEL), F32),
        scratch_shapes=[
            pltpu.VMEM((tm, ATT_OUT), BF16),
            pltpu.VMEM((tm, X_W), BF16),
            pltpu.VMEM((tm, D_MODEL), BF16),
        ],
        compiler_params=_params("parallel", "arbitrary"),
        name="post",
    )(c_act, *o_g, *lse_g, o_mem, sg, sg, sg, wc, wa, wx, wo, x)


def _mix_sample_body(qkv_ref, qx_ref, u_ref, w0_ref, w1_ref, w2_ref, cs_ref, mem_ref, wdw_ref, b_ref, g_ref, bl_ref,
                     o0_ref, o1_ref, o2_ref, l0_ref, l1_ref, l2_ref, c_ref, om_ref, ncs_ref):
    scale = HEAD_DIM ** -0.5
    for gi, (win_ref, o_ref, l_ref) in enumerate(((w0_ref, o0_ref, l0_ref), (w1_ref, o1_ref, l1_ref),
                                                  (w2_ref, o2_ref, l2_ref))):
        lses = []
        for h in range(HEADS_PER_GROUP):
            col = gi * ATT_OUT + h * HEAD_DIM
            q = qkv_ref[:, col:col + HEAD_DIM]
            k_new = qkv_ref[:, ATT_W + col:ATT_W + col + HEAD_DIM]
            v_new = qkv_ref[:, 2 * ATT_W + col:2 * ATT_W + col + HEAD_DIM]
            k_buf, v_buf = win_ref[:, 0, h, :], win_ref[:, 1, h, :]
            s_buf = jnp.sum(k_buf * q, axis=-1, keepdims=True) * scale
            s_new = jnp.sum(k_new * q, axis=-1, keepdims=True) * scale
            m = jnp.maximum(jnp.max(s_buf, axis=0, keepdims=True), s_new)
            p_buf, p_new = jnp.exp(s_buf - m), jnp.exp(s_new - m)
            den = jnp.sum(p_buf, axis=0, keepdims=True) + p_new
            o = (jnp.sum(p_buf * v_buf, axis=0, keepdims=True) + p_new * v_new) / den
            o_ref[:, h * HEAD_DIM:(h + 1) * HEAD_DIM] = o
            lses.append(m + jnp.log(den))
        l_ref[...] = _lane_pack(lses, 1)
    for h in range(X_HEADS):
        hs = slice(h * X_HEAD_DIM, (h + 1) * X_HEAD_DIM)
        s = jnp.sum(mem_ref[:, 0, h, :] * qx_ref[:, hs], axis=-1, keepdims=True) * (X_HEAD_DIM ** -0.5)
        p = jnp.exp(s - jnp.max(s, axis=0, keepdims=True))
        om_ref[:, hs] = jnp.sum(p * mem_ref[:, 1, h, :], axis=0, keepdims=True) / jnp.sum(p, axis=0, keepdims=True)
    hist = CONV_WIDTH - 1
    c = (jnp.sum(cs_ref[...] * wdw_ref[0:hist, :], axis=0, keepdims=True)
         + u_ref[...] * wdw_ref[hist:hist + 1, :] + b_ref[...])
    c_ref[...] = _ln_swish(c, g_ref, bl_ref)
    ncs_ref[0:hist - 1, :] = cs_ref[1:hist, :]
    ncs_ref[hist - 1:hist, :] = u_ref[...]


def _mix_sample(layer, qkv, qx, u, wins, conv_state, mem_kv, w, b, g, bl):
    row = lambda wd: pl.BlockSpec((None, 1, wd), lambda s: (s, 0, 0))
    vec = pl.BlockSpec((1, CONV_CH), lambda s: (0, 0))
    hist = CONV_WIDTH - 1
    win_specs, win_args = [], []
    for (win, dil), arr in zip(GROUPS, wins):
        win_args.append(arr.reshape(DEPTH, DEC_BATCH, win // dil, dil, 2, HEADS_PER_GROUP, HEAD_DIM))
        win_specs.append(pl.BlockSpec((None, None, win // dil, None, 2, HEADS_PER_GROUP, HEAD_DIM),
                                      lambda s: (layer, s, 0, 0, 0, 0, 0)))
    f32_row = lambda wd: jax.ShapeDtypeStruct((DEC_BATCH, 1, wd), F32)
    outs = pl.pallas_call(
        _mix_sample_body,
        grid=(DEC_BATCH,),
        in_specs=[row(QKV_W), row(X_W), row(CONV_CH), *win_specs,
                  pl.BlockSpec((None, None, hist, CONV_CH), lambda s: (layer, s, 0, 0)),
                  pl.BlockSpec((None, None, N_MEM, 2, X_HEADS, X_HEAD_DIM), lambda s: (layer, s, 0, 0, 0, 0)),
                  pl.BlockSpec((CONV_WIDTH, CONV_CH), lambda s: (0, 0)), vec, vec, vec],
        out_specs=[row(ATT_OUT)] * 3 + [row(LANES)] * 3 + [row(CONV_CH), row(X_W),
                                                            pl.BlockSpec((None, hist, CONV_CH), lambda s: (s, 0, 0))],
        out_shape=[f32_row(ATT_OUT)] * 3 + [f32_row(LANES)] * 3 + [
            f32_row(CONV_CH), f32_row(X_W), jax.ShapeDtypeStruct((DEC_BATCH, hist, CONV_CH), F32)],
        compiler_params=_params("parallel"),
        name="mix_sample",
    )(qkv.reshape(DEC_BATCH, 1, QKV_W), qx.reshape(DEC_BATCH, 1, X_W), u.reshape(DEC_BATCH, 1, CONV_CH),
      *win_args, conv_state, mem_kv, w, b, g, bl)
    flat = [o.reshape(DEC_BATCH, -1) for o in outs[:8]]
    return flat[0:3], flat[3:6], flat[6].astype(BF16), flat[7], outs[8]


def _shift_windows_body(*refs):
    n = N_GROUPS
    states, news, outs, sem = refs[0:n], refs[n:2 * n], refs[2 * n:3 * n], refs[3 * n]
    copies = []
    for gi in range(n):
        rows = states[gi].shape[2]
        copies.append(pltpu.make_async_copy(states[gi].at[:, :, pl.ds(1, rows - 1)],
                                            outs[gi].at[:, :, pl.ds(0, rows - 1)], sem.at[2 * gi]))
        copies.append(pltpu.make_async_copy(news[gi], outs[gi].at[:, :, pl.ds(rows - 1, 1)], sem.at[2 * gi + 1]))
    for c in copies:
        c.start()
    for c in copies:
        c.wait()


def _shift_windows(states, news):
    hbm = pl.BlockSpec(memory_space=pl.ANY)
    return pl.pallas_call(
        _shift_windows_body,
        in_specs=[hbm] * (2 * N_GROUPS),
        out_specs=[hbm] * N_GROUPS,
        out_shape=[jax.ShapeDtypeStruct(s.shape, s.dtype) for s in states],
        scratch_shapes=[pltpu.SemaphoreType.DMA((2 * N_GROUPS,))],
        name="shift_windows",
    )(*states, *news)


def kernel(x_prompt, x_sample, state_win0, state_win1, state_win2, state_conv, cache_mem_kv, mem_prompt, w_ffn1_norm, w_ffn1_gate, w_ffn1_up, w_ffn1_down, w_mix_norm, w_in, w_dw, b_dw, g_cln, b_cln, w_conv_out, w_att_out, w_x_out, w_o, w_mem_norm, w_mem_kv, w_ffn2_norm, w_ffn2_gate, w_ffn2_up, w_ffn2_down, w_final_norm):
    state_wins = (state_win0, state_win1, state_win2)
    hp = x_prompt.reshape(BATCH * SEQ, D_MODEL)
    hs = x_sample.reshape(DEC_BATCH, D_MODEL)
    mem = mem_prompt.reshape(BATCH * N_MEM, D_MODEL)
    tab_p = _rope_tables(jnp.arange(SEQ))
    tab_s = _rope_tables(jnp.full((DEC_BATCH,), PAST_LEN))
    vec = lambda a: a.reshape(1, -1)
    g_final = vec(w_final_norm)

    win_p = [[] for _ in GROUPS]
    kv_s = [[] for _ in GROUPS]
    conv_p, conv_s, mem_p = [], [], []
    for l in range(DEPTH):
        last = l == DEPTH - 1
        conv_w = (w_dw[l], vec(b_dw[l]), vec(g_cln[l]), vec(b_cln[l]))
        out_w = tuple(a[l].astype(BF16) for a in (w_conv_out, w_att_out, w_x_out, w_o))

        mkv = _mem_proj(l, mem, vec(w_mem_norm[l]), w_mem_kv)
        mem_p.append(mkv.reshape(BATCH, N_MEM, 2, X_HEADS, X_HEAD_DIM))

        hp, hs = _ffn(l, hp, hs, vec(w_ffn1_norm[l]), w_ffn1_gate, w_ffn1_up, w_ffn1_down)
        (qkv, u, qx, sg), (qkv_s, u_s, qx_s, sg_s) = _in_proj(l, hp, hs, vec(w_mix_norm[l]), w_in, tab_p, tab_s)

        att = [_attn_prompt(qkv, gi, dil) for gi, (_, dil) in enumerate(GROUPS)]
        c_act, conv_tail = _conv_prompt(u, *conv_w)
        o_mem = _cross_prompt(qx, mkv, tq=1024)
        hp = _post(c_act, [a[0] for a in att], [a[1] for a in att], o_mem, sg, *out_w, hp, tm=256)
        qkv3 = qkv.reshape(BATCH, SEQ, QKV_W)
        for gi, (win, _) in enumerate(GROUPS):
            keep = min(win, SEQ)
            cut = lambda c0: qkv3[:, SEQ - keep:, c0 + gi * ATT_OUT:c0 + (gi + 1) * ATT_OUT].reshape(
                BATCH, keep, HEADS_PER_GROUP, HEAD_DIM)
            win_p[gi].append(jnp.stack([cut(ATT_W), cut(2 * ATT_W)], axis=2))
        conv_p.append(conv_tail)

        o_g, lse_g, c_act, o_mem, conv_new = _mix_sample(l, qkv_s, qx_s, u_s, state_wins, state_conv, cache_mem_kv,
                                                         *conv_w)
        hs = _post(c_act, o_g, lse_g, o_mem, sg_s, *out_w, hs, tm=DEC_BATCH)
        for gi in range(N_GROUPS):
            cut = lambda c0: qkv_s[:, c0 + gi * ATT_OUT:c0 + (gi + 1) * ATT_OUT].reshape(
                DEC_BATCH, HEADS_PER_GROUP, HEAD_DIM)
            kv_s[gi].append(jnp.stack([cut(ATT_W), cut(2 * ATT_W)], axis=1))
        conv_s.append(conv_new)

        hp, hs = _ffn(l, hp, hs, vec(w_ffn2_norm[l]), w_ffn2_gate, w_ffn2_up, w_ffn2_down, g_final if last else None)

    st = lambda xs: jnp.stack(xs, axis=0)
    win_s = _shift_windows(state_wins, [st(k)[:, :, None] for k in kv_s])
    return (hp.reshape(BATCH, SEQ, D_MODEL), hs.reshape(DEC_BATCH, 1, D_MODEL),
            st(win_p[0]), st(win_p[1]), st(win_p[2]), st(conv_p), st(mem_p),
            win_s[0], win_s[1], win_s[2], st(conv_s))
```

```python
import functools

import jax
import jax.numpy as jnp
from jax import lax
from jax.experimental import pallas as pl
from jax.experimental.pallas import tpu as pltpu

D_MODEL = 2048
BATCH = 2
SEQ = 4096
DEPTH = 2
DEC_BATCH = 32
PAST_LEN = 8192
HEAD_DIM = 128
HEADS_PER_GROUP = 4
GROUPS = ((128, 1), (512, 4), (2048, 16))
N_GROUPS = len(GROUPS)
ATT_W = N_GROUPS * HEADS_PER_GROUP * HEAD_DIM
ATT_OUT = HEADS_PER_GROUP * HEAD_DIM
BLOCK = 128
ROT_DIM = HEAD_DIM // 4
ROPE_THETA = 500000.0
CONV_CH = 3 * D_MODEL // 4
CONV_WIDTH = 31
X_HEADS = 4
X_HEAD_DIM = D_MODEL // 8
X_W = X_HEADS * X_HEAD_DIM
N_MEM = 256
D_FF = 256 * ((8 * D_MODEL // 3 + 255) // 256)
N_BRANCH = 3
QKV_W = 3 * ATT_W
COL_A = QKV_W
COL_QX = COL_A + 2 * CONV_CH
COL_GATES = COL_QX + X_W
IN_W = COL_GATES + N_BRANCH * D_MODEL
EPS = 1e-6
NEG = -1e30

F32 = jnp.float32
BF16 = jnp.bfloat16

LANES = 128
VMEM_LIMIT_BYTES = 58 * 1024 * 1024
TN = 512
TF = 256
TM = 1024
EDGE_ROWS = 256
CONV_ROWS = 128
CONV_HALO = 32
ATT_TQ = (256, 512, 512)
SHIFT_ROWS = 512


def _params(*sem):
    return pltpu.CompilerParams(dimension_semantics=sem, vmem_limit_bytes=VMEM_LIMIT_BYTES)


def _rms(x, g):
    return x * lax.rsqrt(jnp.mean(x * x, axis=-1, keepdims=True) + EPS) * g


def _row_chunks(n):
    step = min(n, EDGE_ROWS)
    return [slice(r, r + step) for r in range(0, n, step)]


def _norm_rows(x_ref, g_ref, h_ref):
    for r in _row_chunks(x_ref.shape[0]):
        h_ref[r, :] = _rms(x_ref[r, :], g_ref[...]).astype(BF16)


def _dot(a, b):
    return jnp.dot(a, b, preferred_element_type=F32)


def _dot_t(a, b):
    return lax.dot_general(a, b, (((1,), (1,)), ((), ())), preferred_element_type=F32)


def _ffn_body(xp_ref, xs_ref, g_ref, wg_ref, wu_ref, wd_ref, *rest, n_f, final_norm):
    if final_norm:
        gf_ref, op_ref, os_ref, hp_ref, hs_ref = rest
    else:
        op_ref, os_ref, hp_ref, hs_ref = rest
    i, f = pl.program_id(0), pl.program_id(1)
    wg, wu, wd = wg_ref[...], wu_ref[...], wd_ref[...]

    def rows(x_ref, o_ref, h_ref):
        @pl.when(f == 0)
        def _():
            _norm_rows(x_ref, g_ref, h_ref)
            o_ref[...] = jnp.zeros_like(o_ref)

        h = h_ref[...]
        gate = _dot(h, wg)
        act = (gate * jax.nn.sigmoid(gate) * _dot(h, wu)).astype(BF16)
        o_ref[...] += _dot(act, wd)

        @pl.when(f == n_f - 1)
        def _():
            for r in _row_chunks(x_ref.shape[0]):
                y = x_ref[r, :] + 0.5 * o_ref[r, :]
                if final_norm:
                    y = _rms(y, gf_ref[...])
                o_ref[r, :] = y

    rows(xp_ref, op_ref, hp_ref)

    @pl.when(i == 0)
    def _():
        rows(xs_ref, os_ref, hs_ref)


def _ffn(layer, xp, xs, g, wg, wu, wd, gf=None):
    m, ms = xp.shape[0], xs.shape[0]
    n_f = D_FF // TF
    final_norm = gf is not None
    vec = pl.BlockSpec((1, D_MODEL), lambda i, f: (0, 0))
    in_specs = [
        pl.BlockSpec((TM, D_MODEL), lambda i, f: (i, 0), pipeline_mode=pl.Buffered(1)),
        pl.BlockSpec((ms, D_MODEL), lambda i, f: (0, 0)),
        vec,
        pl.BlockSpec((None, D_MODEL, TF), lambda i, f: (layer, 0, f)),
        pl.BlockSpec((None, D_MODEL, TF), lambda i, f: (layer, 0, f)),
        pl.BlockSpec((None, TF, D_MODEL), lambda i, f: (layer, f, 0)),
    ]
    args = [xp, xs, g, wg, wu, wd]
    if final_norm:
        in_specs.append(vec)
        args.append(gf)
    return pl.pallas_call(
        functools.partial(_ffn_body, n_f=n_f, final_norm=final_norm),
        grid=(m // TM, n_f),
        in_specs=in_specs,
        out_specs=[pl.BlockSpec((TM, D_MODEL), lambda i, f: (i, 0)),
                   pl.BlockSpec((ms, D_MODEL), lambda i, f: (0, 0))],
        out_shape=[jax.ShapeDtypeStruct((m, D_MODEL), F32), jax.ShapeDtypeStruct((ms, D_MODEL), F32)],
        scratch_shapes=[pltpu.VMEM((TM, D_MODEL), BF16), pltpu.VMEM((ms, D_MODEL), BF16)],
        compiler_params=_params("arbitrary", "arbitrary"),
        name="ffn",
    )(*args)


def _rope(z, c_ref, s1_ref, s2_ref):
    half = ROT_DIM // 2
    c, s1, s2 = c_ref[...], s1_ref[...], s2_ref[...]
    heads = []
    for h in range(z.shape[-1] // HEAD_DIM):
        zh = z[:, h * HEAD_DIM:(h + 1) * HEAD_DIM]
        heads.append(zh * c + pltpu.roll(zh, HEAD_DIM - half, 1) * s1 + pltpu.roll(zh, half, 1) * s2)
    return jnp.concatenate(heads, axis=1)


def _col_tile(ref, t):
    return ref.at[:, pl.ds(pl.multiple_of(t * TN, TN), TN)]


def _proj_qkvu_body(xp_ref, xs_ref, g_ref, w_ref, w2_ref, cp_ref, s1p_ref, s2p_ref, cs_ref, s1s_ref, s2s_ref,
                    qkvp_ref, up_ref, qkvs_ref, us_ref, hp_ref, hs_ref):
    i, s = pl.program_id(0), pl.program_id(1)
    n_rot, n_qkv = 2 * ATT_W // TN, QKV_W // TN
    w = w_ref[...]

    def rows(x_ref, h_ref, tabs, qkv_ref, u_ref):
        @pl.when(s == 0)
        def _():
            _norm_rows(x_ref, g_ref, h_ref)

        h = h_ref[...]
        z = _dot(h, w)

        @pl.when(s < n_rot)
        def _():
            qkv_ref[...] = _rope(z, *tabs)

        @pl.when((s >= n_rot) & (s < n_qkv))
        def _():
            qkv_ref[...] = z

        @pl.when(s >= n_qkv)
        def _():
            u_ref[...] = z * jax.nn.sigmoid(_dot(h, w2_ref[...]))

    rows(xp_ref, hp_ref, (cp_ref, s1p_ref, s2p_ref), qkvp_ref, up_ref)

    @pl.when(i == 0)
    def _():
        rows(xs_ref, hs_ref, (cs_ref, s1s_ref, s2s_ref),
             _col_tile(qkvs_ref, jnp.minimum(s, n_qkv - 1)), _col_tile(us_ref, jnp.maximum(s - n_qkv, 0)))


def _proj_qxg_body(xp_ref, xs_ref, g_ref, w_ref, qxp_ref, sgp_ref, qxs_ref, sgs_ref, hp_ref, hs_ref):
    i, s = pl.program_id(0), pl.program_id(1)
    n_qx = X_W // TN
    w = w_ref[...]

    def rows(x_ref, h_ref, qx_ref, sg_ref):
        @pl.when(s == 0)
        def _():
            _norm_rows(x_ref, g_ref, h_ref)

        z = _dot(h_ref[...], w)

        @pl.when(s < n_qx)
        def _():
            qx_ref[...] = z

        @pl.when(s >= n_qx)
        def _():
            sg_ref[...] = jax.nn.sigmoid(z)

    rows(xp_ref, hp_ref, qxp_ref, sgp_ref)

    @pl.when(i == 0)
    def _():
        rows(xs_ref, hs_ref, _col_tile(qxs_ref, jnp.minimum(s, n_qx - 1)), _col_tile(sgs_ref, jnp.maximum(s - n_qx, 0)))


def _in_proj(layer, xp, xs, g, w_in, tab_p, tab_s):
    m, ms = xp.shape[0], xs.shape[0]
    n_tab = tab_p[0].shape[0] // TM
    n_qkv, n_u, n_qx, n_g = QKV_W // TN, CONV_CH // TN, X_W // TN, N_BRANCH * D_MODEL // TN
    a0, qx0 = COL_A // TN, COL_QX // TN
    xspec = [pl.BlockSpec((TM, D_MODEL), lambda i, s: (i, 0), pipeline_mode=pl.Buffered(1)),
             pl.BlockSpec((ms, D_MODEL), lambda i, s: (0, 0)),
             pl.BlockSpec((1, D_MODEL), lambda i, s: (0, 0))]
    wspec = lambda fn: pl.BlockSpec((None, D_MODEL, TN), lambda i, s: (layer, 0, fn(s)))
    ptile = lambda fn: pl.BlockSpec((TM, TN), lambda i, s: (i, fn(s)))
    whole = lambda wd: pl.BlockSpec((ms, wd), lambda i, s: (0, 0))
    shp = lambda r, wd: jax.ShapeDtypeStruct((r, wd), F32)
    scratch = [pltpu.VMEM((TM, D_MODEL), BF16), pltpu.VMEM((ms, D_MODEL), BF16)]

    qkv_p, u_p, qkv_s, u_s = pl.pallas_call(
        _proj_qkvu_body,
        grid=(m // TM, n_qkv + n_u),
        in_specs=xspec + [
            wspec(lambda s: s),
            wspec(lambda s: jnp.clip(s + n_u, a0 + n_u, a0 + 2 * n_u - 1)),
            *[pl.BlockSpec((TM, HEAD_DIM), lambda i, s: (i % n_tab, 0)) for _ in range(3)],
            *[pl.BlockSpec((ms, HEAD_DIM), lambda i, s: (0, 0)) for _ in range(3)],
        ],
        out_specs=[ptile(lambda s: jnp.minimum(s, n_qkv - 1)), ptile(lambda s: jnp.maximum(s - n_qkv, 0)),
                   whole(QKV_W), whole(CONV_CH)],
        out_shape=[shp(m, QKV_W), shp(m, CONV_CH), shp(ms, QKV_W), shp(ms, CONV_CH)],
        scratch_shapes=scratch,
        compiler_params=_params("arbitrary", "arbitrary"),
        name="proj_qkvu",
    )(xp, xs, g, w_in, w_in, *tab_p, *tab_s)

    qx_p, sg_p, qx_s, sg_s = pl.pallas_call(
        _proj_qxg_body,
        grid=(m // TM, n_qx + n_g),
        in_specs=xspec + [wspec(lambda s: qx0 + s)],
        out_specs=[ptile(lambda s: jnp.minimum(s, n_qx - 1)), ptile(lambda s: jnp.maximum(s - n_qx, 0)),
                   whole(X_W), whole(N_BRANCH * D_MODEL)],
        out_shape=[shp(m, X_W), shp(m, N_BRANCH * D_MODEL), shp(ms, X_W), shp(ms, N_BRANCH * D_MODEL)],
        scratch_shapes=scratch,
        compiler_params=_params("arbitrary", "arbitrary"),
        name="proj_qxg",
    )(xp, xs, g, w_in)
    return (qkv_p, u_p, qx_p, sg_p), (qkv_s, u_s, qx_s, sg_s)


def _mem_proj_body(x_ref, g_ref, w_ref, o_ref, h_ref):
    @pl.when(pl.program_id(0) == 0)
    def _():
        h_ref[...] = _rms(x_ref[...], g_ref[...]).astype(BF16)

    o_ref[...] = _dot(h_ref[...], w_ref[...])


def _mem_proj(layer, x, g, w):
    m, n = x.shape[0], w.shape[-1]
    return pl.pallas_call(
        _mem_proj_body,
        grid=(n // TN,),
        in_specs=[pl.BlockSpec((m, D_MODEL), lambda j: (0, 0)),
                  pl.BlockSpec((1, D_MODEL), lambda j: (0, 0)),
                  pl.BlockSpec((None, D_MODEL, TN), lambda j: (layer, 0, j))],
        out_specs=pl.BlockSpec((m, TN), lambda j: (0, j)),
        out_shape=jax.ShapeDtypeStruct((m, n), F32),
        scratch_shapes=[pltpu.VMEM((m, D_MODEL), BF16)],
        compiler_params=_params("arbitrary"),
        name="mem_proj",
    )(x, g, w)


def _rope_tables(pos):
    half = ROT_DIM // 2
    inv = ROPE_THETA ** (-jnp.arange(half, dtype=F32) / half)
    ang = pos.astype(F32)[:, None] * inv[None, :]
    cos, sin = jnp.cos(ang), jnp.sin(ang)
    n = pos.shape[0]
    zeros = jnp.zeros((n, half), F32)
    rest0 = jnp.zeros((n, HEAD_DIM - ROT_DIM), F32)
    c = jnp.concatenate([cos, cos, jnp.ones((n, HEAD_DIM - ROT_DIM), F32)], axis=1)
    s1 = jnp.concatenate([-sin, zeros, rest0], axis=1)
    s2 = jnp.concatenate([zeros, sin, rest0], axis=1)
    return c, s1, s2


def _softmax_parts(s):
    m = jnp.max(s, axis=-1, keepdims=True)
    p = jnp.exp(s - m)
    return m, p, jnp.sum(p, axis=-1, keepdims=True)


def _lane_pack(cols, rows):
    lane = lax.broadcasted_iota(jnp.int32, (rows, LANES), 1)
    out = jnp.zeros((rows, LANES), F32)
    for h, c in enumerate(cols):
        out = jnp.where(lane == h, c, out)
    return out


def _attn_prompt_body(q_ref, k_ref, v_ref, o_ref, lse_ref, *, win, dil, tq):
    t0 = pl.program_id(1) * tq
    span = win + tq
    start = pl.multiple_of(jnp.maximum(t0 - win, 0), min(win, tq))
    qpos = t0 + lax.broadcasted_iota(jnp.int32, (tq, span), 0)
    kpos = start + lax.broadcasted_iota(jnp.int32, (tq, span), 1)
    dist = qpos - kpos
    mask = (dist >= 0) & (dist <= win) & ((dist & (dil - 1)) == 0)
    lses = []
    for h in range(HEADS_PER_GROUP):
        hs = slice(h * HEAD_DIM, (h + 1) * HEAD_DIM)
        q = q_ref[:, hs].astype(BF16)
        k = k_ref[pl.ds(start, span), hs].astype(BF16)
        v = v_ref[pl.ds(start, span), hs].astype(BF16)
        s = jnp.where(mask, _dot_t(q, k) * (HEAD_DIM ** -0.5), NEG)
        m, p, den = _softmax_parts(s)
        o_ref[:, hs] = _dot(p.astype(BF16), v) / den
        lses.append(m + jnp.log(den))
    lse_ref[...] = _lane_pack(lses, tq)


def _attn_prompt(qkv, gi):
    win, dil = GROUPS[gi]
    assert dil & (dil - 1) == 0 and win + ATT_TQ[gi] <= SEQ
    tq = ATT_TQ[gi]
    n_t = SEQ // tq
    kt, vt = ATT_W // ATT_OUT + gi, 2 * ATT_W // ATT_OUT + gi
    return pl.pallas_call(
        functools.partial(_attn_prompt_body, win=win, dil=dil, tq=tq),
        grid=(BATCH, n_t),
        in_specs=[
            pl.BlockSpec((tq, ATT_OUT), lambda b, t: (b * n_t + t, gi)),
            pl.BlockSpec((SEQ, ATT_OUT), lambda b, t: (b, kt), pipeline_mode=pl.Buffered(1)),
            pl.BlockSpec((SEQ, ATT_OUT), lambda b, t: (b, vt), pipeline_mode=pl.Buffered(1)),
        ],
        out_specs=[
            pl.BlockSpec((tq, ATT_OUT), lambda b, t: (b * n_t + t, 0)),
            pl.BlockSpec((tq, LANES), lambda b, t: (b * n_t + t, 0)),
        ],
        out_shape=[
            jax.ShapeDtypeStruct((BATCH * SEQ, ATT_OUT), F32),
            jax.ShapeDtypeStruct((BATCH * SEQ, LANES), F32),
        ],
        compiler_params=_params("parallel", "arbitrary"),
        name="attn_prompt_g%d" % gi,
    )(qkv, qkv, qkv)


def _ln_swish(c, g_ref, b_ref):
    mu = jnp.mean(c, axis=-1, keepdims=True)
    xc = c - mu
    y = xc * lax.rsqrt(jnp.mean(xc * xc, axis=-1, keepdims=True) + EPS) * g_ref[...] + b_ref[...]
    return y * jax.nn.sigmoid(y)


def _conv_prompt_body(u_ref, halo_ref, w_ref, b_ref, g_ref, bl_ref, o_ref, tail_ref, ext_ref, c_ref, *, n_t):
    t = pl.program_id(1)
    halo = halo_ref[...]
    ext_ref[0:CONV_HALO, :] = jnp.where(t > 0, halo, jnp.zeros_like(halo))
    ext_ref[CONV_HALO:, :] = u_ref[...]
    first = CONV_HALO - (CONV_WIDTH - 1)
    for c in range(CONV_CH // LANES):
        cs = slice(c * LANES, (c + 1) * LANES)
        acc = jnp.zeros((CONV_ROWS, LANES), F32)
        for k in range(CONV_WIDTH):
            acc = acc + ext_ref[first + k:first + k + CONV_ROWS, cs] * w_ref[k:k + 1, cs]
        c_ref[:, cs] = acc
    o_ref[...] = _ln_swish(c_ref[...] + b_ref[...], g_ref, bl_ref).astype(BF16)

    @pl.when(t == n_t - 1)
    def _():
        tail_ref[...] = u_ref[CONV_ROWS - (CONV_WIDTH - 1):, :]


def _conv_prompt(u, w, b, g, bl):
    n_t = SEQ // CONV_ROWS
    hpb = CONV_ROWS // CONV_HALO
    vec = pl.BlockSpec((1, CONV_CH), lambda b_, t: (0, 0))
    return pl.pallas_call(
        functools.partial(_conv_prompt_body, n_t=n_t),
        grid=(BATCH, n_t),
        in_specs=[
            pl.BlockSpec((CONV_ROWS, CONV_CH), lambda b_, t: (b_ * n_t + t, 0)),
            pl.BlockSpec((CONV_HALO, CONV_CH), lambda b_, t: (jnp.maximum((b_ * n_t + t) * hpb - 1, 0), 0)),
            pl.BlockSpec((CONV_WIDTH, CONV_CH), lambda b_, t: (0, 0)),
            vec, vec, vec,
        ],
        out_specs=[
            pl.BlockSpec((CONV_ROWS, CONV_CH), lambda b_, t: (b_ * n_t + t, 0)),
            pl.BlockSpec((None, CONV_WIDTH - 1, CONV_CH), lambda b_, t: (b_, 0, 0)),
        ],
        out_shape=[
            jax.ShapeDtypeStruct((BATCH * SEQ, CONV_CH), BF16),
            jax.ShapeDtypeStruct((BATCH, CONV_WIDTH - 1, CONV_CH), F32),
        ],
        scratch_shapes=[
            pltpu.VMEM((CONV_HALO + CONV_ROWS, CONV_CH), F32),
            pltpu.VMEM((CONV_ROWS, CONV_CH), F32),
        ],
        compiler_params=_params("parallel", "arbitrary"),
        name="conv_prompt",
    )(u, u, w, b, g, bl)


def _cross_prompt_body(q_ref, k_ref, v_ref, o_ref):
    for h in range(TN // X_HEAD_DIM):
        hs = slice(h * X_HEAD_DIM, (h + 1) * X_HEAD_DIM)
        s = _dot_t(q_ref[:, hs].astype(BF16), k_ref[:, hs].astype(BF16)) * (X_HEAD_DIM ** -0.5)
        _, p, den = _softmax_parts(s)
        o_ref[:, hs] = _dot(p.astype(BF16), v_ref[:, hs].astype(BF16)) / den


def _cross_prompt(qx, mkv, *, tq):
    n_t = SEQ // tq
    n_hp = X_W // TN
    mkv3 = mkv.reshape(BATCH, N_MEM, 2 * X_W)
    return pl.pallas_call(
        _cross_prompt_body,
        grid=(BATCH, n_t, n_hp),
        in_specs=[
            pl.BlockSpec((tq, TN), lambda b, t, hp: (b * n_t + t, hp)),
            pl.BlockSpec((None, N_MEM, TN), lambda b, t, hp: (b, 0, hp)),
            pl.BlockSpec((None, N_MEM, TN), lambda b, t, hp: (b, 0, n_hp + hp)),
        ],
        out_specs=pl.BlockSpec((tq, TN), lambda b, t, hp: (b * n_t + t, hp)),
        out_shape=jax.ShapeDtypeStruct((BATCH * SEQ, X_W), F32),
        compiler_params=_params("parallel", "parallel", "arbitrary"),
        name="cross_prompt",
    )(qx, mkv3, mkv3)


def _post_body(c_ref, o0_ref, o1_ref, o2_ref, l0_ref, l1_ref, l2_ref, om_ref, ga_ref, gb_ref, gc_ref,
               wc_ref, wa_ref, wx_ref, wo_ref, x_ref, out_ref, oatt_ref, omem_ref, merged_ref, *, n_n):
    n = pl.program_id(1)

    @pl.when(n == 0)
    def _():
        for h in range(HEADS_PER_GROUP):
            hs = slice(h * HEAD_DIM, (h + 1) * HEAD_DIM)
            l0, l1, l2 = l0_ref[:, h:h + 1], l1_ref[:, h:h + 1], l2_ref[:, h:h + 1]
            m = jnp.maximum(jnp.maximum(l0, l1), l2)
            e0, e1, e2 = jnp.exp(l0 - m), jnp.exp(l1 - m), jnp.exp(l2 - m)
            mix = (e0 * o0_ref[:, hs] + e1 * o1_ref[:, hs] + e2 * o2_ref[:, hs]) / (e0 + e1 + e2)
            oatt_ref[:, hs] = mix.astype(BF16)
        omem_ref[...] = om_ref[...].astype(BF16)

    merged = (ga_ref[...] * _dot(c_ref[...], wc_ref[...])
              + gb_ref[...] * _dot(oatt_ref[...], wa_ref[...])
              + gc_ref[...] * _dot(omem_ref[...], wx_ref[...]))
    merged_ref[:, pl.ds(pl.multiple_of(n * TN, TN), TN)] = merged.astype(BF16)

    @pl.when(n == n_n - 1)
    def _():
        out_ref[...] = x_ref[...] + _dot(merged_ref[...], wo_ref[...])


def _post(c_act, o_g, lse_g, o_mem, sg, wc, wa, wx, wo, x, *, tm):
    m = x.shape[0]
    n_n = D_MODEL // TN
    row = lambda w: pl.BlockSpec((tm, w), lambda i, n: (i, 0))
    gate = lambda br: pl.BlockSpec((tm, TN), lambda i, n: (i, br * n_n + n))
    wcol = lambda k: pl.BlockSpec((k, TN), lambda i, n: (0, n))
    return pl.pallas_call(
        functools.partial(_post_body, n_n=n_n),
        grid=(m // tm, n_n),
        in_specs=[
            row(CONV_CH), row(ATT_OUT), row(ATT_OUT), row(ATT_OUT), row(LANES), row(LANES), row(LANES), row(X_W),
            gate(0), gate(1), gate(2),
            wcol(CONV_CH), wcol(ATT_OUT), wcol(X_W),
            pl.BlockSpec((D_MODEL, D_MODEL), lambda i, n: (0, 0), pipeline_mode=pl.Buffered(1)),
            row(D_MODEL),
        ],
        out_specs=row(D_MODEL),
        out_shape=jax.ShapeDtypeStruct((m, D_MODEL), F32),
        scratch_shapes=[
            pltpu.VMEM((tm, ATT_OUT), BF16),
            pltpu.VMEM((tm, X_W), BF16),
            pltpu.VMEM((tm, D_MODEL), BF16),
        ],
        compiler_params=_params("parallel", "arbitrary"),
        name="post",
    )(c_act, *o_g, *lse_g, o_mem, sg, sg, sg, wc, wa, wx, wo, x)


def _mix_sample_body(qkv_ref, qx_ref, u_ref, w0_ref, w1_ref, w2_ref, cs_ref, mem_ref, wdw_ref, b_ref, g_ref, bl_ref,
                     o0_ref, o1_ref, o2_ref, l0_ref, l1_ref, l2_ref, c_ref, om_ref, ncs_ref):
    scale = HEAD_DIM ** -0.5
    for gi, (win_ref, o_ref, l_ref) in enumerate(((w0_ref, o0_ref, l0_ref), (w1_ref, o1_ref, l1_ref),
                                                  (w2_ref, o2_ref, l2_ref))):
        lses = []
        for h in range(HEADS_PER_GROUP):
            col = gi * ATT_OUT + h * HEAD_DIM
            q = qkv_ref[:, col:col + HEAD_DIM]
            k_new = qkv_ref[:, ATT_W + col:ATT_W + col + HEAD_DIM]
            v_new = qkv_ref[:, 2 * ATT_W + col:2 * ATT_W + col + HEAD_DIM]
            k_buf, v_buf = win_ref[:, 0, h, :], win_ref[:, 1, h, :]
            s_buf = jnp.sum(k_buf * q, axis=-1, keepdims=True) * scale
            s_new = jnp.sum(k_new * q, axis=-1, keepdims=True) * scale
            m = jnp.maximum(jnp.max(s_buf, axis=0, keepdims=True), s_new)
            p_buf, p_new = jnp.exp(s_buf - m), jnp.exp(s_new - m)
            den = jnp.sum(p_buf, axis=0, keepdims=True) + p_new
            o = (jnp.sum(p_buf * v_buf, axis=0, keepdims=True) + p_new * v_new) / den
            o_ref[:, h * HEAD_DIM:(h + 1) * HEAD_DIM] = o
            lses.append(m + jnp.log(den))
        l_ref[...] = _lane_pack(lses, 1)
    for h in range(X_HEADS):
        hs = slice(h * X_HEAD_DIM, (h + 1) * X_HEAD_DIM)
        s = jnp.sum(mem_ref[:, 0, h, :] * qx_ref[:, hs], axis=-1, keepdims=True) * (X_HEAD_DIM ** -0.5)
        p = jnp.exp(s - jnp.max(s, axis=0, keepdims=True))
        om_ref[:, hs] = jnp.sum(p * mem_ref[:, 1, h, :], axis=0, keepdims=True) / jnp.sum(p, axis=0, keepdims=True)
    hist = CONV_WIDTH - 1
    c = (jnp.sum(cs_ref[...] * wdw_ref[0:hist, :], axis=0, keepdims=True)
         + u_ref[...] * wdw_ref[hist:hist + 1, :] + b_ref[...])
    c_ref[...] = _ln_swish(c, g_ref, bl_ref)
    ncs_ref[0:hist - 1, :] = cs_ref[1:hist, :]
    ncs_ref[hist - 1:hist, :] = u_ref[...]


def _mix_sample(layer, qkv, qx, u, wins, conv_state, mem_kv, w, b, g, bl):
    row = lambda wd: pl.BlockSpec((None, 1, wd), lambda s: (s, 0, 0))
    vec = pl.BlockSpec((1, CONV_CH), lambda s: (0, 0))
    hist = CONV_WIDTH - 1
    win_specs, win_args = [], []
    for (win, dil), arr in zip(GROUPS, wins):
        win_args.append(arr.reshape(DEPTH, DEC_BATCH, win // dil, dil, 2, HEADS_PER_GROUP, HEAD_DIM))
        win_specs.append(pl.BlockSpec((None, None, win // dil, None, 2, HEADS_PER_GROUP, HEAD_DIM),
                                      lambda s: (layer, s, 0, 0, 0, 0, 0)))
    f32_row = lambda wd: jax.ShapeDtypeStruct((DEC_BATCH, 1, wd), F32)
    outs = pl.pallas_call(
        _mix_sample_body,
        grid=(DEC_BATCH,),
        in_specs=[row(QKV_W), row(X_W), row(CONV_CH), *win_specs,
                  pl.BlockSpec((None, None, hist, CONV_CH), lambda s: (layer, s, 0, 0)),
                  pl.BlockSpec((None, None, N_MEM, 2, X_HEADS, X_HEAD_DIM), lambda s: (layer, s, 0, 0, 0, 0)),
                  pl.BlockSpec((CONV_WIDTH, CONV_CH), lambda s: (0, 0)), vec, vec, vec],
        out_specs=[row(ATT_OUT)] * 3 + [row(LANES)] * 3 + [row(CONV_CH), row(X_W),
                                                            pl.BlockSpec((None, hist, CONV_CH), lambda s: (s, 0, 0))],
        out_shape=[f32_row(ATT_OUT)] * 3 + [f32_row(LANES)] * 3 + [
            f32_row(CONV_CH), f32_row(X_W), jax.ShapeDtypeStruct((DEC_BATCH, hist, CONV_CH), F32)],
        compiler_params=_params("parallel"),
        name="mix_sample",
    )(qkv.reshape(DEC_BATCH, 1, QKV_W), qx.reshape(DEC_BATCH, 1, X_W), u.reshape(DEC_BATCH, 1, CONV_CH),
      *win_args, conv_state, mem_kv, w, b, g, bl)
    flat = [o.reshape(DEC_BATCH, -1) for o in outs[:8]]
    return flat[0:3], flat[3:6], flat[6].astype(BF16), flat[7], outs[8]


def _shift_window_body(x_ref, nxt_ref, new_ref, o_ref, *, n_c):
    rows = x_ref.shape[0]
    o_ref[0:rows - 1] = x_ref[1:rows]
    last = pl.program_id(2) == n_c - 1
    o_ref[rows - 1:rows] = jnp.where(last, new_ref[...], nxt_ref[...])


def _shift_window(state, new):
    rows = state.shape[2]
    rc = min(rows, SHIFT_ROWS)
    n_c = rows // rc
    tail = state.shape[3:]
    zeros = (0,) * len(tail)
    one = pl.BlockSpec((None, None, 1) + tail, lambda l, b, c: (l, b, jnp.minimum((c + 1) * rc, rows - 1)) + zeros)
    return pl.pallas_call(
        functools.partial(_shift_window_body, n_c=n_c),
        grid=(DEPTH, DEC_BATCH, n_c),
        in_specs=[pl.BlockSpec((None, None, rc) + tail, lambda l, b, c: (l, b, c) + zeros),
                  one,
                  pl.BlockSpec((None, None, 1) + tail, lambda l, b, c: (l, b, 0) + zeros)],
        out_specs=pl.BlockSpec((None, None, rc) + tail, lambda l, b, c: (l, b, c) + zeros),
        out_shape=jax.ShapeDtypeStruct(state.shape, state.dtype),
        compiler_params=_params("parallel", "parallel", "arbitrary"),
        name="shift_window",
    )(state, state, new)


def kernel(x_prompt, x_sample, state_win0, state_win1, state_win2, state_conv, cache_mem_kv, mem_prompt, w_ffn1_norm, w_ffn1_gate, w_ffn1_up, w_ffn1_down, w_mix_norm, w_in, w_dw, b_dw, g_cln, b_cln, w_conv_out, w_att_out, w_x_out, w_o, w_mem_norm, w_mem_kv, w_ffn2_norm, w_ffn2_gate, w_ffn2_up, w_ffn2_down, w_final_norm):
    state_wins = (state_win0, state_win1, state_win2)
    hp = x_prompt.reshape(BATCH * SEQ, D_MODEL)
    hs = x_sample.reshape(DEC_BATCH, D_MODEL)
    mem = mem_prompt.reshape(BATCH * N_MEM, D_MODEL)
    tab_p = _rope_tables(jnp.arange(SEQ))
    tab_s = _rope_tables(jnp.full((DEC_BATCH,), PAST_LEN))
    vec = lambda a: a.reshape(1, -1)
    g_final = vec(w_final_norm)
    ffn1_w = tuple(a.astype(BF16) for a in (w_ffn1_gate, w_ffn1_up, w_ffn1_down))
    ffn2_w = tuple(a.astype(BF16) for a in (w_ffn2_gate, w_ffn2_up, w_ffn2_down))
    w_in_bf, w_mem_kv_bf = w_in.astype(BF16), w_mem_kv.astype(BF16)

    win_p = [[] for _ in GROUPS]
    kv_s = [[] for _ in GROUPS]
    conv_p, conv_s, mem_p = [], [], []
    for l in range(DEPTH):
        last = l == DEPTH - 1
        conv_w = (w_dw[l], vec(b_dw[l]), vec(g_cln[l]), vec(b_cln[l]))
        out_w = tuple(a[l].astype(BF16) for a in (w_conv_out, w_att_out, w_x_out, w_o))

        mkv = _mem_proj(l, mem, vec(w_mem_norm[l]), w_mem_kv_bf)
        mem_p.append(mkv.reshape(BATCH, N_MEM, 2, X_HEADS, X_HEAD_DIM))

        hp, hs = _ffn(l, hp, hs, vec(w_ffn1_norm[l]), *ffn1_w)
        (qkv, u, qx, sg), (qkv_s, u_s, qx_s, sg_s) = _in_proj(l, hp, hs, vec(w_mix_norm[l]), w_in_bf, tab_p, tab_s)

        att = [_attn_prompt(qkv, gi) for gi in range(N_GROUPS)]
        c_act, conv_tail = _conv_prompt(u, *conv_w)
        o_mem = _cross_prompt(qx, mkv, tq=1024)
        hp = _post(c_act, [a[0] for a in att], [a[1] for a in att], o_mem, sg, *out_w, hp, tm=256)
        qkv3 = qkv.reshape(BATCH, SEQ, QKV_W)
        for gi, (win, _) in enumerate(GROUPS):
            keep = min(win, SEQ)
            cut = lambda c0: qkv3[:, SEQ - keep:, c0 + gi * ATT_OUT:c0 + (gi + 1) * ATT_OUT].reshape(
                BATCH, keep, HEADS_PER_GROUP, HEAD_DIM)
            win_p[gi].append(jnp.stack([cut(ATT_W), cut(2 * ATT_W)], axis=2))
        conv_p.append(conv_tail)

        o_g, lse_g, c_act, o_mem, conv_new = _mix_sample(l, qkv_s, qx_s, u_s, state_wins, state_conv, cache_mem_kv,
                                                         *conv_w)
        hs = _post(c_act, o_g, lse_g, o_mem, sg_s, *out_w, hs, tm=DEC_BATCH)
        for gi in range(N_GROUPS):
            cut = lambda c0: qkv_s[:, c0 + gi * ATT_OUT:c0 + (gi + 1) * ATT_OUT].reshape(
                DEC_BATCH, HEADS_PER_GROUP, HEAD_DIM)
            kv_s[gi].append(jnp.stack([cut(ATT_W), cut(2 * ATT_W)], axis=1))
        conv_s.append(conv_new)

        hp, hs = _ffn(l, hp, hs, vec(w_ffn2_norm[l]), *ffn2_w, g_final if last else None)

    st = lambda xs: jnp.stack(xs, axis=0)
    win_s = [_shift_window(s, st(k)[:, :, None]) for s, k in zip(state_wins, kv_s)]
    return (hp.reshape(BATCH, SEQ, D_MODEL), hs.reshape(DEC_BATCH, 1, D_MODEL),
            st(win_p[0]), st(win_p[1]), st(win_p[2]), st(conv_p), st(mem_p),
            win_s[0], win_s[1], win_s[2], st(conv_s))
```

```python
import functools

import jax
import jax.numpy as jnp
from jax import lax
from jax.experimental import pallas as pl
from jax.experimental.pallas import tpu as pltpu

D_MODEL = 2048
BATCH = 2
SEQ = 4096
DEPTH = 2
DEC_BATCH = 32
PAST_LEN = 8192
HEAD_DIM = 128
HEADS_PER_GROUP = 4
GROUPS = ((128, 1), (512, 4), (2048, 16))
N_GROUPS = len(GROUPS)
ATT_W = N_GROUPS * HEADS_PER_GROUP * HEAD_DIM
ATT_OUT = HEADS_PER_GROUP * HEAD_DIM
BLOCK = 128
ROT_DIM = HEAD_DIM // 4
ROPE_THETA = 500000.0
CONV_CH = 3 * D_MODEL // 4
CONV_WIDTH = 31
X_HEADS = 4
X_HEAD_DIM = D_MODEL // 8
X_W = X_HEADS * X_HEAD_DIM
N_MEM = 256
D_FF = 256 * ((8 * D_MODEL // 3 + 255) // 256)
N_BRANCH = 3
QKV_W = 3 * ATT_W
COL_A = QKV_W
COL_QX = COL_A + 2 * CONV_CH
COL_GATES = COL_QX + X_W
IN_W = COL_GATES + N_BRANCH * D_MODEL
EPS = 1e-6
NEG = -1e30

F32 = jnp.float32
BF16 = jnp.bfloat16

LANES = 128
SUBLANES = 8
VMEM_LIMIT_BYTES = 58 * 1024 * 1024
TN = 512
TF = 512
TF_FINAL = 256
TM = 1024
EDGE_ROWS = 256
CONV_ROWS = 128
CONV_HALO = 32
CONV_PASS = 64
LN_ROWS = 16
ATT_TQ = (256, 512, 512)
SHIFT_ROWS = 512


def _params(*sem):
    return pltpu.CompilerParams(dimension_semantics=sem, vmem_limit_bytes=VMEM_LIMIT_BYTES)


def _rms(x, g):
    return x * lax.rsqrt(jnp.mean(x * x, axis=-1, keepdims=True) + EPS) * g


def _row_chunks(n):
    step = min(n, EDGE_ROWS)
    return [slice(r, r + step) for r in range(0, n, step)]


def _norm_rows(x_ref, g_ref, h_ref):
    for r in _row_chunks(x_ref.shape[0]):
        h_ref[r, :] = _rms(x_ref[r, :], g_ref[...]).astype(BF16)


def _dot(a, b):
    return jnp.dot(a, b, preferred_element_type=F32)


def _dot_t(a, b):
    return lax.dot_general(a, b, (((1,), (1,)), ((), ())), preferred_element_type=F32)


def _ffn_body(xp_ref, xs_ref, g_ref, wg_ref, wu_ref, wd_ref, *rest, n_f, final_norm):
    if final_norm:
        gf_ref, op_ref, os_ref, hp_ref, hs_ref = rest
    else:
        op_ref, os_ref, hp_ref, hs_ref = rest
    i, f = pl.program_id(0), pl.program_id(1)
    wg, wu, wd = wg_ref[...], wu_ref[...], wd_ref[...]

    def rows(x_ref, o_ref, h_ref):
        @pl.when(f == 0)
        def _():
            _norm_rows(x_ref, g_ref, h_ref)
            o_ref[...] = jnp.zeros_like(o_ref)

        h = h_ref[...]
        gate = _dot(h, wg)
        act = (gate * jax.nn.sigmoid(gate) * _dot(h, wu)).astype(BF16)
        o_ref[...] += _dot(act, wd)

        @pl.when(f == n_f - 1)
        def _():
            for r in _row_chunks(x_ref.shape[0]):
                y = x_ref[r, :] + 0.5 * o_ref[r, :]
                if final_norm:
                    y = _rms(y, gf_ref[...])
                o_ref[r, :] = y

    rows(xp_ref, op_ref, hp_ref)

    @pl.when(i == 0)
    def _():
        rows(xs_ref, os_ref, hs_ref)


def _ffn(layer, xp, xs, g, wg, wu, wd, gf=None):
    m, ms = xp.shape[0], xs.shape[0]
    final_norm = gf is not None
    tf = TF_FINAL if final_norm else TF
    n_f = D_FF // tf
    vec = pl.BlockSpec((1, D_MODEL), lambda i, f: (0, 0))
    in_specs = [
        pl.BlockSpec((TM, D_MODEL), lambda i, f: (i, 0), pipeline_mode=pl.Buffered(1)),
        pl.BlockSpec((ms, D_MODEL), lambda i, f: (0, 0)),
        vec,
        pl.BlockSpec((None, D_MODEL, tf), lambda i, f: (layer, 0, f)),
        pl.BlockSpec((None, D_MODEL, tf), lambda i, f: (layer, 0, f)),
        pl.BlockSpec((None, tf, D_MODEL), lambda i, f: (layer, f, 0)),
    ]
    args = [xp, xs, g, wg, wu, wd]
    if final_norm:
        in_specs.append(vec)
        args.append(gf)
    return pl.pallas_call(
        functools.partial(_ffn_body, n_f=n_f, final_norm=final_norm),
        grid=(m // TM, n_f),
        in_specs=in_specs,
        out_specs=[pl.BlockSpec((TM, D_MODEL), lambda i, f: (i, 0)),
                   pl.BlockSpec((ms, D_MODEL), lambda i, f: (0, 0))],
        out_shape=[jax.ShapeDtypeStruct((m, D_MODEL), F32), jax.ShapeDtypeStruct((ms, D_MODEL), F32)],
        scratch_shapes=[pltpu.VMEM((TM, D_MODEL), BF16), pltpu.VMEM((ms, D_MODEL), BF16)],
        compiler_params=_params("arbitrary", "arbitrary"),
        name="ffn",
    )(*args)


def _rope(z, c_ref, s1_ref, s2_ref):
    half = ROT_DIM // 2
    c, s1, s2 = c_ref[...], s1_ref[...], s2_ref[...]
    heads = []
    for h in range(z.shape[-1] // HEAD_DIM):
        zh = z[:, h * HEAD_DIM:(h + 1) * HEAD_DIM]
        heads.append(zh * c + pltpu.roll(zh, HEAD_DIM - half, 1) * s1 + pltpu.roll(zh, half, 1) * s2)
    return jnp.concatenate(heads, axis=1)


def _col_tile(ref, t):
    return ref.at[:, pl.ds(pl.multiple_of(t * TN, TN), TN)]


def _proj_qkvu_body(xp_ref, xs_ref, g_ref, w_ref, w2_ref, cp_ref, s1p_ref, s2p_ref, cs_ref, s1s_ref, s2s_ref,
                    qkvp_ref, up_ref, qkvs_ref, us_ref, hp_ref, hs_ref):
    i, s = pl.program_id(0), pl.program_id(1)
    n_rot, n_qkv = 2 * ATT_W // TN, QKV_W // TN
    w = w_ref[...]

    def rows(x_ref, h_ref, tabs, qkv_ref, u_ref):
        @pl.when(s == 0)
        def _():
            _norm_rows(x_ref, g_ref, h_ref)

        h = h_ref[...]
        z = _dot(h, w)

        @pl.when(s < n_rot)
        def _():
            qkv_ref[...] = _rope(z, *tabs)

        @pl.when((s >= n_rot) & (s < n_qkv))
        def _():
            qkv_ref[...] = z

        @pl.when(s >= n_qkv)
        def _():
            u_ref[...] = z * jax.nn.sigmoid(_dot(h, w2_ref[...]))

    rows(xp_ref, hp_ref, (cp_ref, s1p_ref, s2p_ref), qkvp_ref, up_ref)

    @pl.when(i == 0)
    def _():
        rows(xs_ref, hs_ref, (cs_ref, s1s_ref, s2s_ref),
             _col_tile(qkvs_ref, jnp.minimum(s, n_qkv - 1)), _col_tile(us_ref, jnp.maximum(s - n_qkv, 0)))


def _proj_qxg_body(xp_ref, xs_ref, g_ref, w_ref, qxp_ref, sgp_ref, qxs_ref, sgs_ref, hp_ref, hs_ref):
    i, s = pl.program_id(0), pl.program_id(1)
    n_qx = X_W // TN
    w = w_ref[...]

    def rows(x_ref, h_ref, qx_ref, sg_ref):
        @pl.when(s == 0)
        def _():
            _norm_rows(x_ref, g_ref, h_ref)

        z = _dot(h_ref[...], w)

        @pl.when(s < n_qx)
        def _():
            qx_ref[...] = z

        @pl.when(s >= n_qx)
        def _():
            sg_ref[...] = jax.nn.sigmoid(z)

    rows(xp_ref, hp_ref, qxp_ref, sgp_ref)

    @pl.when(i == 0)
    def _():
        rows(xs_ref, hs_ref, _col_tile(qxs_ref, jnp.minimum(s, n_qx - 1)), _col_tile(sgs_ref, jnp.maximum(s - n_qx, 0)))


def _in_proj(layer, xp, xs, g, w_in, tab_p, tab_s):
    m, ms = xp.shape[0], xs.shape[0]
    n_tab = tab_p[0].shape[0] // TM
    n_qkv, n_u, n_qx, n_g = QKV_W // TN, CONV_CH // TN, X_W // TN, N_BRANCH * D_MODEL // TN
    a0, qx0 = COL_A // TN, COL_QX // TN
    xspec = [pl.BlockSpec((TM, D_MODEL), lambda i, s: (i, 0)),
             pl.BlockSpec((ms, D_MODEL), lambda i, s: (0, 0)),
             pl.BlockSpec((1, D_MODEL), lambda i, s: (0, 0))]
    wspec = lambda fn: pl.BlockSpec((None, D_MODEL, TN), lambda i, s: (layer, 0, fn(s)))
    ptile = lambda fn: pl.BlockSpec((TM, TN), lambda i, s: (i, fn(s)))
    whole = lambda wd: pl.BlockSpec((ms, wd), lambda i, s: (0, 0))
    shp = lambda r, wd: jax.ShapeDtypeStruct((r, wd), F32)
    scratch = [pltpu.VMEM((TM, D_MODEL), BF16), pltpu.VMEM((ms, D_MODEL), BF16)]

    qkv_p, u_p, qkv_s, u_s = pl.pallas_call(
        _proj_qkvu_body,
        grid=(m // TM, n_qkv + n_u),
        in_specs=xspec + [
            wspec(lambda s: s),
            wspec(lambda s: jnp.clip(s + n_u, a0 + n_u, a0 + 2 * n_u - 1)),
            *[pl.BlockSpec((TM, HEAD_DIM), lambda i, s: (i % n_tab, 0)) for _ in range(3)],
            *[pl.BlockSpec((ms, HEAD_DIM), lambda i, s: (0, 0)) for _ in range(3)],
        ],
        out_specs=[ptile(lambda s: jnp.minimum(s, n_qkv - 1)), ptile(lambda s: jnp.maximum(s - n_qkv, 0)),
                   whole(QKV_W), whole(CONV_CH)],
        out_shape=[shp(m, QKV_W), shp(m, CONV_CH), shp(ms, QKV_W), shp(ms, CONV_CH)],
        scratch_shapes=scratch,
        compiler_params=_params("arbitrary", "arbitrary"),
        name="proj_qkvu",
    )(xp, xs, g, w_in, w_in, *tab_p, *tab_s)

    qx_p, sg_p, qx_s, sg_s = pl.pallas_call(
        _proj_qxg_body,
        grid=(m // TM, n_qx + n_g),
        in_specs=xspec + [wspec(lambda s: qx0 + s)],
        out_specs=[ptile(lambda s: jnp.minimum(s, n_qx - 1)), ptile(lambda s: jnp.maximum(s - n_qx, 0)),
                   whole(X_W), whole(N_BRANCH * D_MODEL)],
        out_shape=[shp(m, X_W), shp(m, N_BRANCH * D_MODEL), shp(ms, X_W), shp(ms, N_BRANCH * D_MODEL)],
        scratch_shapes=scratch,
        compiler_params=_params("arbitrary", "arbitrary"),
        name="proj_qxg",
    )(xp, xs, g, w_in)
    return (qkv_p, u_p, qx_p, sg_p), (qkv_s, u_s, qx_s, sg_s)


def _mem_proj_body(x_ref, g_ref, w_ref, o_ref, h_ref):
    @pl.when(pl.program_id(0) == 0)
    def _():
        h_ref[...] = _rms(x_ref[...], g_ref[...]).astype(BF16)

    o_ref[...] = _dot(h_ref[...], w_ref[...])


def _mem_proj(layer, x, g, w):
    m, n = x.shape[0], w.shape[-1]
    return pl.pallas_call(
        _mem_proj_body,
        grid=(n // TN,),
        in_specs=[pl.BlockSpec((m, D_MODEL), lambda j: (0, 0)),
                  pl.BlockSpec((1, D_MODEL), lambda j: (0, 0)),
                  pl.BlockSpec((None, D_MODEL, TN), lambda j: (layer, 0, j))],
        out_specs=pl.BlockSpec((m, TN), lambda j: (0, j)),
        out_shape=jax.ShapeDtypeStruct((m, n), F32),
        scratch_shapes=[pltpu.VMEM((m, D_MODEL), BF16)],
        compiler_params=_params("arbitrary"),
        name="mem_proj",
    )(x, g, w)


def _rope_tables(pos):
    half = ROT_DIM // 2
    inv = ROPE_THETA ** (-jnp.arange(half, dtype=F32) / half)
    ang = pos.astype(F32)[:, None] * inv[None, :]
    cos, sin = jnp.cos(ang), jnp.sin(ang)
    n = pos.shape[0]
    zeros = jnp.zeros((n, half), F32)
    rest0 = jnp.zeros((n, HEAD_DIM - ROT_DIM), F32)
    c = jnp.concatenate([cos, cos, jnp.ones((n, HEAD_DIM - ROT_DIM), F32)], axis=1)
    s1 = jnp.concatenate([-sin, zeros, rest0], axis=1)
    s2 = jnp.concatenate([zeros, sin, rest0], axis=1)
    return c, s1, s2


def _softmax_parts(s):
    m = jnp.max(s, axis=-1, keepdims=True)
    p = jnp.exp(s - m)
    return m, p, jnp.sum(p, axis=-1, keepdims=True)


def _lane_pack(cols, rows):
    lane = lax.broadcasted_iota(jnp.int32, (rows, LANES), 1)
    out = jnp.zeros((rows, LANES), F32)
    for h, c in enumerate(cols):
        out = jnp.where(lane == h, c, out)
    return out


def _attn_prompt_body(q_ref, k_ref, v_ref, o_ref, lse_ref, *, win, dil, tq):
    t0 = pl.program_id(1) * tq
    span = win + tq
    start = pl.multiple_of(jnp.maximum(t0 - win, 0), min(win, tq))
    qpos = t0 + lax.broadcasted_iota(jnp.int32, (tq, span), 0)
    kpos = start + lax.broadcasted_iota(jnp.int32, (tq, span), 1)
    dist = qpos - kpos
    mask = (dist >= 0) & (dist <= win) & ((dist & (dil - 1)) == 0)
    lses = []
    for h in range(HEADS_PER_GROUP):
        hs = slice(h * HEAD_DIM, (h + 1) * HEAD_DIM)
        q = (q_ref[:, hs] * (HEAD_DIM ** -0.5)).astype(BF16)
        k = k_ref[pl.ds(start, span), hs].astype(BF16)
        v = v_ref[pl.ds(start, span), hs].astype(BF16)
        s = jnp.where(mask, _dot_t(q, k), NEG)
        m, p, den = _softmax_parts(s)
        o_ref[:, hs] = _dot(p.astype(BF16), v) / den
        lses.append(m + jnp.log(den))
    lse_ref[...] = _lane_pack(lses, tq)


def _attn_prompt(qkv, gi):
    win, dil = GROUPS[gi]
    assert dil & (dil - 1) == 0 and win + ATT_TQ[gi] <= SEQ
    tq = ATT_TQ[gi]
    n_t = SEQ // tq
    kt, vt = ATT_W // ATT_OUT + gi, 2 * ATT_W // ATT_OUT + gi
    return pl.pallas_call(
        functools.partial(_attn_prompt_body, win=win, dil=dil, tq=tq),
        grid=(BATCH, n_t),
        in_specs=[
            pl.BlockSpec((tq, ATT_OUT), lambda b, t: (b * n_t + t, gi)),
            pl.BlockSpec((SEQ, ATT_OUT), lambda b, t: (b, kt), pipeline_mode=pl.Buffered(1)),
            pl.BlockSpec((SEQ, ATT_OUT), lambda b, t: (b, vt), pipeline_mode=pl.Buffered(1)),
        ],
        out_specs=[
            pl.BlockSpec((tq, ATT_OUT), lambda b, t: (b * n_t + t, 0)),
            pl.BlockSpec((tq, LANES), lambda b, t: (b * n_t + t, 0)),
        ],
        out_shape=[
            jax.ShapeDtypeStruct((BATCH * SEQ, ATT_OUT), F32),
            jax.ShapeDtypeStruct((BATCH * SEQ, LANES), F32),
        ],
        compiler_params=_params("parallel", "arbitrary"),
        name="attn_prompt_g%d" % gi,
    )(qkv, qkv, qkv)


def _ln_swish(c, g_ref, b_ref):
    mu = jnp.mean(c, axis=-1, keepdims=True)
    xc = c - mu
    y = xc * lax.rsqrt(jnp.mean(xc * xc, axis=-1, keepdims=True) + EPS) * g_ref[...] + b_ref[...]
    return y * jax.nn.sigmoid(y)


def _conv_prompt_body(u_ref, halo_ref, w_ref, b_ref, g_ref, bl_ref, o_ref, tail_ref, ext_ref, sh_ref, c_ref, *, n_t):
    t = pl.program_id(1)
    halo = halo_ref[...]
    ext_ref[0:CONV_HALO, :] = jnp.where(t > 0, halo, jnp.zeros_like(halo))
    ext_ref[CONV_HALO:, :] = u_ref[...]
    first = CONV_HALO - (CONV_WIDTH - 1)
    n_sh = CONV_HALO + CONV_ROWS - SUBLANES
    for c in range(CONV_CH // LANES):
        cs = slice(c * LANES, (c + 1) * LANES)
        for s in range(1, SUBLANES):
            sh_ref[s, 0:n_sh, :] = ext_ref[s:s + n_sh, cs]
        for r0 in range(0, CONV_ROWS, CONV_PASS):
            acc = jnp.zeros((CONV_PASS, LANES), F32)
            for k in range(CONV_WIDTH):
                s, q = (first + k) % SUBLANES, r0 + (first + k) // SUBLANES * SUBLANES
                rows = ext_ref[q:q + CONV_PASS, cs] if s == 0 else sh_ref[s, q:q + CONV_PASS, :]
                acc = acc + rows * w_ref[k:k + 1, cs]
            c_ref[r0:r0 + CONV_PASS, cs] = acc
    for r0 in range(0, CONV_ROWS, LN_ROWS):
        rs = slice(r0, r0 + LN_ROWS)
        o_ref[rs, :] = _ln_swish(c_ref[rs, :] + b_ref[...], g_ref, bl_ref).astype(BF16)

    @pl.when(t == n_t - 1)
    def _():
        tail_ref[...] = u_ref[CONV_ROWS - (CONV_WIDTH - 1):, :]


def _conv_prompt(u, w, b, g, bl):
    n_t = SEQ // CONV_ROWS
    hpb = CONV_ROWS // CONV_HALO
    vec = pl.BlockSpec((1, CONV_CH), lambda b_, t: (0, 0))
    return pl.pallas_call(
        functools.partial(_conv_prompt_body, n_t=n_t),
        grid=(BATCH, n_t),
        in_specs=[
            pl.BlockSpec((CONV_ROWS, CONV_CH), lambda b_, t: (b_ * n_t + t, 0)),
            pl.BlockSpec((CONV_HALO, CONV_CH), lambda b_, t: (jnp.maximum((b_ * n_t + t) * hpb - 1, 0), 0)),
            pl.BlockSpec((CONV_WIDTH, CONV_CH), lambda b_, t: (0, 0)),
            vec, vec, vec,
        ],
        out_specs=[
            pl.BlockSpec((CONV_ROWS, CONV_CH), lambda b_, t: (b_ * n_t + t, 0)),
            pl.BlockSpec((None, CONV_WIDTH - 1, CONV_CH), lambda b_, t: (b_, 0, 0)),
        ],
        out_shape=[
            jax.ShapeDtypeStruct((BATCH * SEQ, CONV_CH), BF16),
            jax.ShapeDtypeStruct((BATCH, CONV_WIDTH - 1, CONV_CH), F32),
        ],
        scratch_shapes=[
            pltpu.VMEM((CONV_HALO + CONV_ROWS, CONV_CH), F32),
            pltpu.VMEM((SUBLANES, CONV_HALO + CONV_ROWS, LANES), F32),
            pltpu.VMEM((CONV_ROWS, CONV_CH), F32),
        ],
        compiler_params=_params("parallel", "arbitrary"),
        name="conv_prompt",
    )(u, u, w, b, g, bl)


def _cross_prompt_body(q_ref, k_ref, v_ref, o_ref):
    for h in range(TN // X_HEAD_DIM):
        hs = slice(h * X_HEAD_DIM, (h + 1) * X_HEAD_DIM)
        s = _dot_t(q_ref[:, hs].astype(BF16), k_ref[:, hs].astype(BF16)) * (X_HEAD_DIM ** -0.5)
        _, p, den = _softmax_parts(s)
        o_ref[:, hs] = _dot(p.astype(BF16), v_ref[:, hs].astype(BF16)) / den


def _cross_prompt(qx, mkv, *, tq):
    n_t = SEQ // tq
    n_hp = X_W // TN
    mkv3 = mkv.reshape(BATCH, N_MEM, 2 * X_W)
    return pl.pallas_call(
        _cross_prompt_body,
        grid=(BATCH, n_t, n_hp),
        in_specs=[
            pl.BlockSpec((tq, TN), lambda b, t, hp: (b * n_t + t, hp)),
            pl.BlockSpec((None, N_MEM, TN), lambda b, t, hp: (b, 0, hp)),
            pl.BlockSpec((None, N_MEM, TN), lambda b, t, hp: (b, 0, n_hp + hp)),
        ],
        out_specs=pl.BlockSpec((tq, TN), lambda b, t, hp: (b * n_t + t, hp)),
        out_shape=jax.ShapeDtypeStruct((BATCH * SEQ, X_W), F32),
        compiler_params=_params("parallel", "parallel", "arbitrary"),
        name="cross_prompt",
    )(qx, mkv3, mkv3)


def _post_body(c_ref, o0_ref, o1_ref, o2_ref, l0_ref, l1_ref, l2_ref, om_ref, ga_ref, gb_ref, gc_ref,
               wc_ref, wa_ref, wx_ref, wo_ref, x_ref, out_ref, oatt_ref, omem_ref, merged_ref, *, n_n):
    n = pl.program_id(1)

    @pl.when(n == 0)
    def _():
        for h in range(HEADS_PER_GROUP):
            hs = slice(h * HEAD_DIM, (h + 1) * HEAD_DIM)
            l0, l1, l2 = l0_ref[:, h:h + 1], l1_ref[:, h:h + 1], l2_ref[:, h:h + 1]
            m = jnp.maximum(jnp.maximum(l0, l1), l2)
            e0, e1, e2 = jnp.exp(l0 - m), jnp.exp(l1 - m), jnp.exp(l2 - m)
            mix = (e0 * o0_ref[:, hs] + e1 * o1_ref[:, hs] + e2 * o2_ref[:, hs]) / (e0 + e1 + e2)
            oatt_ref[:, hs] = mix.astype(BF16)
        omem_ref[...] = om_ref[...].astype(BF16)

    merged = (ga_ref[...] * _dot(c_ref[...], wc_ref[...])
              + gb_ref[...] * _dot(oatt_ref[...], wa_ref[...])
              + gc_ref[...] * _dot(omem_ref[...], wx_ref[...]))
    merged_ref[:, pl.ds(pl.multiple_of(n * TN, TN), TN)] = merged.astype(BF16)

    @pl.when(n == n_n - 1)
    def _():
        out_ref[...] = x_ref[...] + _dot(merged_ref[...], wo_ref[...])


def _post(c_act, o_g, lse_g, o_mem, sg, wc, wa, wx, wo, x, *, tm):
    m = x.shape[0]
    n_n = D_MODEL // TN
    row = lambda w: pl.BlockSpec((tm, w), lambda i, n: (i, 0))
    gate = lambda br: pl.BlockSpec((tm, TN), lambda i, n: (i, br * n_n + n))
    wcol = lambda k: pl.BlockSpec((k, TN), lambda i, n: (0, n))
    return pl.pallas_call(
        functools.partial(_post_body, n_n=n_n),
        grid=(m // tm, n_n),
        in_specs=[
            row(CONV_CH), row(ATT_OUT), row(ATT_OUT), row(ATT_OUT), row(LANES), row(LANES), row(LANES), row(X_W),
            gate(0), gate(1), gate(2),
            wcol(CONV_CH), wcol(ATT_OUT), wcol(X_W),
            pl.BlockSpec((D_MODEL, D_MODEL), lambda i, n: (0, 0), pipeline_mode=pl.Buffered(1)),
            row(D_MODEL),
        ],
        out_specs=row(D_MODEL),
        out_shape=jax.ShapeDtypeStruct((m, D_MODEL), F32),
        scratch_shapes=[
            pltpu.VMEM((tm, ATT_OUT), BF16),
            pltpu.VMEM((tm, X_W), BF16),
            pltpu.VMEM((tm, D_MODEL), BF16),
        ],
        compiler_params=_params("parallel", "arbitrary"),
        name="post",
    )(c_act, *o_g, *lse_g, o_mem, sg, sg, sg, wc, wa, wx, wo, x)


def _mix_sample_body(qkv_ref, qx_ref, u_ref, w0_ref, w1_ref, w2_ref, cs_ref, mem_ref, wdw_ref, b_ref, g_ref, bl_ref,
                     o0_ref, o1_ref, o2_ref, l0_ref, l1_ref, l2_ref, c_ref, om_ref, ncs_ref):
    scale = HEAD_DIM ** -0.5
    for gi, (win_ref, o_ref, l_ref) in enumerate(((w0_ref, o0_ref, l0_ref), (w1_ref, o1_ref, l1_ref),
                                                  (w2_ref, o2_ref, l2_ref))):
        lses = []
        for h in range(HEADS_PER_GROUP):
            col = gi * ATT_OUT + h * HEAD_DIM
            q = qkv_ref[:, col:col + HEAD_DIM]
            k_new = qkv_ref[:, ATT_W + col:ATT_W + col + HEAD_DIM]
            v_new = qkv_ref[:, 2 * ATT_W + col:2 * ATT_W + col + HEAD_DIM]
            k_buf, v_buf = win_ref[:, 0, h, :], win_ref[:, 1, h, :]
            s_buf = jnp.sum(k_buf * q, axis=-1, keepdims=True) * scale
            s_new = jnp.sum(k_new * q, axis=-1, keepdims=True) * scale
            m = jnp.maximum(jnp.max(s_buf, axis=0, keepdims=True), s_new)
            p_buf, p_new = jnp.exp(s_buf - m), jnp.exp(s_new - m)
            den = jnp.sum(p_buf, axis=0, keepdims=True) + p_new
            o = (jnp.sum(p_buf * v_buf, axis=0, keepdims=True) + p_new * v_new) / den
            o_ref[:, h * HEAD_DIM:(h + 1) * HEAD_DIM] = o
            lses.append(m + jnp.log(den))
        l_ref[...] = _lane_pack(lses, 1)
    for h in range(X_HEADS):
        hs = slice(h * X_HEAD_DIM, (h + 1) * X_HEAD_DIM)
        s = jnp.sum(mem_ref[:, 0, h, :] * qx_ref[:, hs], axis=-1, keepdims=True) * (X_HEAD_DIM ** -0.5)
        p = jnp.exp(s - jnp.max(s, axis=0, keepdims=True))
        om_ref[:, hs] = jnp.sum(p * mem_ref[:, 1, h, :], axis=0, keepdims=True) / jnp.sum(p, axis=0, keepdims=True)
    hist = CONV_WIDTH - 1
    c = (jnp.sum(cs_ref[...] * wdw_ref[0:hist, :], axis=0, keepdims=True)
         + u_ref[...] * wdw_ref[hist:hist + 1, :] + b_ref[...])
    c_ref[...] = _ln_swish(c, g_ref, bl_ref)
    ncs_ref[0:hist - 1, :] = cs_ref[1:hist, :]
    ncs_ref[hist - 1:hist, :] = u_ref[...]


def _mix_sample(layer, qkv, qx, u, wins, conv_state, mem_kv, w, b, g, bl):
    row = lambda wd: pl.BlockSpec((None, 1, wd), lambda s: (s, 0, 0))
    vec = pl.BlockSpec((1, CONV_CH), lambda s: (0, 0))
    hist = CONV_WIDTH - 1
    win_specs, win_args = [], []
    for (win, dil), arr in zip(GROUPS, wins):
        win_args.append(arr.reshape(DEPTH, DEC_BATCH, win // dil, dil, 2, HEADS_PER_GROUP, HEAD_DIM))
        win_specs.append(pl.BlockSpec((None, None, win // dil, None, 2, HEADS_PER_GROUP, HEAD_DIM),
                                      lambda s: (layer, s, 0, 0, 0, 0, 0)))
    f32_row = lambda wd: jax.ShapeDtypeStruct((DEC_BATCH, 1, wd), F32)
    outs = pl.pallas_call(
        _mix_sample_body,
        grid=(DEC_BATCH,),
        in_specs=[row(QKV_W), row(X_W), row(CONV_CH), *win_specs,
                  pl.BlockSpec((None, None, hist, CONV_CH), lambda s: (layer, s, 0, 0)),
                  pl.BlockSpec((None, None, N_MEM, 2, X_HEADS, X_HEAD_DIM), lambda s: (layer, s, 0, 0, 0, 0)),
                  pl.BlockSpec((CONV_WIDTH, CONV_CH), lambda s: (0, 0)), vec, vec, vec],
        out_specs=[row(ATT_OUT)] * 3 + [row(LANES)] * 3 + [row(CONV_CH), row(X_W),
                                                            pl.BlockSpec((None, hist, CONV_CH), lambda s: (s, 0, 0))],
        out_shape=[f32_row(ATT_OUT)] * 3 + [f32_row(LANES)] * 3 + [
            f32_row(CONV_CH), f32_row(X_W), jax.ShapeDtypeStruct((DEC_BATCH, hist, CONV_CH), F32)],
        compiler_params=_params("parallel"),
        name="mix_sample",
    )(qkv.reshape(DEC_BATCH, 1, QKV_W), qx.reshape(DEC_BATCH, 1, X_W), u.reshape(DEC_BATCH, 1, CONV_CH),
      *win_args, conv_state, mem_kv, w, b, g, bl)
    flat = [o.reshape(DEC_BATCH, -1) for o in outs[:8]]
    return flat[0:3], flat[3:6], flat[6].astype(BF16), flat[7], outs[8]


def _shift_window_body(x_ref, nxt_ref, new_ref, o_ref, *, n_c):
    rows = x_ref.shape[0]
    o_ref[0:rows - 1] = x_ref[1:rows]
    last = pl.program_id(2) == n_c - 1
    o_ref[rows - 1:rows] = jnp.where(last, new_ref[...], nxt_ref[...])


def _shift_window(state, new):
    rows = state.shape[2]
    rc = min(rows, SHIFT_ROWS)
    n_c = rows // rc
    tail = state.shape[3:]
    zeros = (0,) * len(tail)
    one = pl.BlockSpec((None, None, 1) + tail, lambda l, b, c: (l, b, jnp.minimum((c + 1) * rc, rows - 1)) + zeros)
    return pl.pallas_call(
        functools.partial(_shift_window_body, n_c=n_c),
        grid=(DEPTH, DEC_BATCH, n_c),
        in_specs=[pl.BlockSpec((None, None, rc) + tail, lambda l, b, c: (l, b, c) + zeros),
                  one,
                  pl.BlockSpec((None, None, 1) + tail, lambda l, b, c: (l, b, 0) + zeros)],
        out_specs=pl.BlockSpec((None, None, rc) + tail, lambda l, b, c: (l, b, c) + zeros),
        out_shape=jax.ShapeDtypeStruct(state.shape, state.dtype),
        compiler_params=_params("parallel", "parallel", "arbitrary"),
        name="shift_window",
    )(state, state, new)


def kernel(x_prompt, x_sample, state_win0, state_win1, state_win2, state_conv, cache_mem_kv, mem_prompt, w_ffn1_norm, w_ffn1_gate, w_ffn1_up, w_ffn1_down, w_mix_norm, w_in, w_dw, b_dw, g_cln, b_cln, w_conv_out, w_att_out, w_x_out, w_o, w_mem_norm, w_mem_kv, w_ffn2_norm, w_ffn2_gate, w_ffn2_up, w_ffn2_down, w_final_norm):
    state_wins = (state_win0, state_win1, state_win2)
    hp = x_prompt.reshape(BATCH * SEQ, D_MODEL)
    hs = x_sample.reshape(DEC_BATCH, D_MODEL)
    mem = mem_prompt.reshape(BATCH * N_MEM, D_MODEL)
    tab_p = _rope_tables(jnp.arange(SEQ))
    tab_s = _rope_tables(jnp.full((DEC_BATCH,), PAST_LEN))
    vec = lambda a: a.reshape(1, -1)
    g_final = vec(w_final_norm)
    ffn1_w = tuple(a.astype(BF16) for a in (w_ffn1_gate, w_ffn1_up, w_ffn1_down))
    ffn2_w = tuple(a.astype(BF16) for a in (w_ffn2_gate, w_ffn2_up, w_ffn2_down))
    w_in_bf, w_mem_kv_bf = w_in.astype(BF16), w_mem_kv.astype(BF16)

    win_p = [[] for _ in GROUPS]
    kv_s = [[] for _ in GROUPS]
    conv_p, conv_s, mem_p = [], [], []
    for l in range(DEPTH):
        last = l == DEPTH - 1
        conv_w = (w_dw[l], vec(b_dw[l]), vec(g_cln[l]), vec(b_cln[l]))
        out_w = tuple(a[l].astype(BF16) for a in (w_conv_out, w_att_out, w_x_out, w_o))

        mkv = _mem_proj(l, mem, vec(w_mem_norm[l]), w_mem_kv_bf)
        mem_p.append(mkv.reshape(BATCH, N_MEM, 2, X_HEADS, X_HEAD_DIM))

        hp, hs = _ffn(l, hp, hs, vec(w_ffn1_norm[l]), *ffn1_w)
        (qkv, u, qx, sg), (qkv_s, u_s, qx_s, sg_s) = _in_proj(l, hp, hs, vec(w_mix_norm[l]), w_in_bf, tab_p, tab_s)

        att = [_attn_prompt(qkv, gi) for gi in range(N_GROUPS)]
        c_act, conv_tail = _conv_prompt(u, *conv_w)
        o_mem = _cross_prompt(qx, mkv, tq=1024)
        hp = _post(c_act, [a[0] for a in att], [a[1] for a in att], o_mem, sg, *out_w, hp, tm=256)
        qkv3 = qkv.reshape(BATCH, SEQ, QKV_W)
        for gi, (win, _) in enumerate(GROUPS):
            keep = min(win, SEQ)
            cut = lambda c0: qkv3[:, SEQ - keep:, c0 + gi * ATT_OUT:c0 + (gi + 1) * ATT_OUT].reshape(
                BATCH, keep, HEADS_PER_GROUP, HEAD_DIM)
            win_p[gi].append(jnp.stack([cut(ATT_W), cut(2 * ATT_W)], axis=2))
        conv_p.append(conv_tail)

        o_g, lse_g, c_act, o_mem, conv_new = _mix_sample(l, qkv_s, qx_s, u_s, state_wins, state_conv, cache_mem_kv,
                                                         *conv_w)
        hs = _post(c_act, o_g, lse_g, o_mem, sg_s, *out_w, hs, tm=DEC_BATCH)
        for gi in range(N_GROUPS):
            cut = lambda c0: qkv_s[:, c0 + gi * ATT_OUT:c0 + (gi + 1) * ATT_OUT].reshape(
                DEC_BATCH, HEADS_PER_GROUP, HEAD_DIM)
            kv_s[gi].append(jnp.stack([cut(ATT_W), cut(2 * ATT_W)], axis=1))
        conv_s.append(conv_new)

        hp, hs = _ffn(l, hp, hs, vec(w_ffn2_norm[l]), *ffn2_w, g_final if last else None)

    st = lambda xs: jnp.stack(xs, axis=0)
    win_s = [_shift_window(s, st(k)[:, :, None]) for s, k in zip(state_wins, kv_s)]
    return (hp.reshape(BATCH, SEQ, D_MODEL), hs.reshape(DEC_BATCH, 1, D_MODEL),
            st(win_p[0]), st(win_p[1]), st(win_p[2]), st(conv_p), st(mem_p),
            win_s[0], win_s[1], win_s[2], st(conv_s))
```

```python
import functools

import jax
import jax.numpy as jnp
from jax import lax
from jax.experimental import pallas as pl
from jax.experimental.pallas import tpu as pltpu

D_MODEL = 2048
BATCH = 2
SEQ = 4096
DEPTH = 2
DEC_BATCH = 32
PAST_LEN = 8192
HEAD_DIM = 128
HEADS_PER_GROUP = 4
GROUPS = ((128, 1), (512, 4), (2048, 16))
N_GROUPS = len(GROUPS)
ATT_W = N_GROUPS * HEADS_PER_GROUP * HEAD_DIM
ATT_OUT = HEADS_PER_GROUP * HEAD_DIM
BLOCK = 128
ROT_DIM = HEAD_DIM // 4
ROPE_THETA = 500000.0
CONV_CH = 3 * D_MODEL // 4
CONV_WIDTH = 31
X_HEADS = 4
X_HEAD_DIM = D_MODEL // 8
X_W = X_HEADS * X_HEAD_DIM
N_MEM = 256
D_FF = 256 * ((8 * D_MODEL // 3 + 255) // 256)
N_BRANCH = 3
QKV_W = 3 * ATT_W
COL_A = QKV_W
COL_QX = COL_A + 2 * CONV_CH
COL_GATES = COL_QX + X_W
IN_W = COL_GATES + N_BRANCH * D_MODEL
EPS = 1e-6
NEG = -1e30

F32 = jnp.float32
BF16 = jnp.bfloat16

LANES = 128
SUBLANES = 8
VMEM_LIMIT_BYTES = 58 * 1024 * 1024
TN = 512
TF = 512
TF_FINAL = 256
TM = 1024
EDGE_ROWS = 256
CONV_ROWS = 128
CONV_HALO = 32
CONV_PASS = 64
LN_ROWS = 16
ATT_TQ = (256, 512, 512)
SHIFT_ROWS = 512


def _params(*sem):
    return pltpu.CompilerParams(dimension_semantics=sem, vmem_limit_bytes=VMEM_LIMIT_BYTES)


def _rms(x, g):
    return x * lax.rsqrt(jnp.mean(x * x, axis=-1, keepdims=True) + EPS) * g


def _row_chunks(n):
    step = min(n, EDGE_ROWS)
    return [slice(r, r + step) for r in range(0, n, step)]


def _norm_rows(x_ref, g_ref, h_ref):
    for r in _row_chunks(x_ref.shape[0]):
        h_ref[r, :] = _rms(x_ref[r, :], g_ref[...]).astype(BF16)


def _dot(a, b):
    return jnp.dot(a, b, preferred_element_type=F32)


def _dot_t(a, b):
    return lax.dot_general(a, b, (((1,), (1,)), ((), ())), preferred_element_type=F32)


def _ffn_body(xp_ref, xs_ref, g_ref, wg_ref, wu_ref, wd_ref, *rest, n_f, final_norm):
    if final_norm:
        gf_ref, op_ref, os_ref, hp_ref, hs_ref = rest
    else:
        op_ref, os_ref, hp_ref, hs_ref = rest
    i, f = pl.program_id(0), pl.program_id(1)
    wg, wu, wd = wg_ref[...], wu_ref[...], wd_ref

    def rows(x_ref, o_ref, h_ref):
        @pl.when(f == 0)
        def _():
            _norm_rows(x_ref, g_ref, h_ref)
            o_ref[...] = jnp.zeros_like(o_ref)

        h = h_ref[...]
        gate = _dot(h, wg)
        act = (gate * jax.nn.sigmoid(gate) * _dot(h, wu)).astype(BF16)
        for n in range(0, D_MODEL, TN):
            o_ref[:, n:n + TN] += _dot(act, wd[:, n:n + TN])

        @pl.when(f == n_f - 1)
        def _():
            for r in _row_chunks(x_ref.shape[0]):
                y = x_ref[r, :] + 0.5 * o_ref[r, :]
                if final_norm:
                    y = _rms(y, gf_ref[...])
                o_ref[r, :] = y

    rows(xp_ref, op_ref, hp_ref)

    @pl.when(i == 0)
    def _():
        rows(xs_ref, os_ref, hs_ref)


def _ffn(layer, xp, xs, g, wg, wu, wd, gf=None):
    m, ms = xp.shape[0], xs.shape[0]
    final_norm = gf is not None
    tf = TF_FINAL if final_norm else TF
    n_f = D_FF // tf
    vec = pl.BlockSpec((1, D_MODEL), lambda i, f: (0, 0))
    in_specs = [
        pl.BlockSpec((TM, D_MODEL), lambda i, f: (i, 0), pipeline_mode=pl.Buffered(1)),
        pl.BlockSpec((ms, D_MODEL), lambda i, f: (0, 0)),
        vec,
        pl.BlockSpec((None, D_MODEL, tf), lambda i, f: (layer, 0, f)),
        pl.BlockSpec((None, D_MODEL, tf), lambda i, f: (layer, 0, f)),
        pl.BlockSpec((None, tf, D_MODEL), lambda i, f: (layer, f, 0)),
    ]
    args = [xp, xs, g, wg, wu, wd]
    if final_norm:
        in_specs.append(vec)
        args.append(gf)
    return pl.pallas_call(
        functools.partial(_ffn_body, n_f=n_f, final_norm=final_norm),
        grid=(m // TM, n_f),
        in_specs=in_specs,
        out_specs=[pl.BlockSpec((TM, D_MODEL), lambda i, f: (i, 0)),
                   pl.BlockSpec((ms, D_MODEL), lambda i, f: (0, 0))],
        out_shape=[jax.ShapeDtypeStruct((m, D_MODEL), F32), jax.ShapeDtypeStruct((ms, D_MODEL), F32)],
        scratch_shapes=[pltpu.VMEM((TM, D_MODEL), BF16), pltpu.VMEM((ms, D_MODEL), BF16)],
        compiler_params=_params("arbitrary", "arbitrary"),
        name="ffn",
    )(*args)


def _rope(z, c_ref, s1_ref, s2_ref):
    half = ROT_DIM // 2
    c, s1, s2 = c_ref[...], s1_ref[...], s2_ref[...]
    heads = []
    for h in range(z.shape[-1] // HEAD_DIM):
        zh = z[:, h * HEAD_DIM:(h + 1) * HEAD_DIM]
        heads.append(zh * c + pltpu.roll(zh, HEAD_DIM - half, 1) * s1 + pltpu.roll(zh, half, 1) * s2)
    return jnp.concatenate(heads, axis=1)


def _col_tile(ref, t):
    return ref.at[:, pl.ds(pl.multiple_of(t * TN, TN), TN)]


def _proj_qkvu_body(xp_ref, xs_ref, g_ref, w_ref, w2_ref, cp_ref, s1p_ref, s2p_ref, cs_ref, s1s_ref, s2s_ref,
                    qkvp_ref, up_ref, qkvs_ref, us_ref, hp_ref, hs_ref):
    i, s = pl.program_id(0), pl.program_id(1)
    n_rot, n_qkv = 2 * ATT_W // TN, QKV_W // TN
    w = w_ref[...]

    def rows(x_ref, h_ref, tabs, qkv_ref, u_ref):
        @pl.when(s == 0)
        def _():
            _norm_rows(x_ref, g_ref, h_ref)

        h = h_ref[...]
        z = _dot(h, w)

        @pl.when(s < n_rot)
        def _():
            qkv_ref[...] = _rope(z, *tabs)

        @pl.when((s >= n_rot) & (s < n_qkv))
        def _():
            qkv_ref[...] = z

        @pl.when(s >= n_qkv)
        def _():
            u_ref[...] = z * jax.nn.sigmoid(_dot(h, w2_ref[...]))

    rows(xp_ref, hp_ref, (cp_ref, s1p_ref, s2p_ref), qkvp_ref, up_ref)

    @pl.when(i == 0)
    def _():
        rows(xs_ref, hs_ref, (cs_ref, s1s_ref, s2s_ref),
             _col_tile(qkvs_ref, jnp.minimum(s, n_qkv - 1)), _col_tile(us_ref, jnp.maximum(s - n_qkv, 0)))


def _proj_qxg_body(xp_ref, xs_ref, g_ref, w_ref, qxp_ref, sgp_ref, qxs_ref, sgs_ref, hp_ref, hs_ref):
    i, s = pl.program_id(0), pl.program_id(1)
    n_qx = X_W // TN
    w = w_ref[...]

    def rows(x_ref, h_ref, qx_ref, sg_ref):
        @pl.when(s == 0)
        def _():
            _norm_rows(x_ref, g_ref, h_ref)

        z = _dot(h_ref[...], w)

        @pl.when(s < n_qx)
        def _():
            qx_ref[...] = z

        @pl.when(s >= n_qx)
        def _():
            sg_ref[...] = jax.nn.sigmoid(z)

    rows(xp_ref, hp_ref, qxp_ref, sgp_ref)

    @pl.when(i == 0)
    def _():
        rows(xs_ref, hs_ref, _col_tile(qxs_ref, jnp.minimum(s, n_qx - 1)), _col_tile(sgs_ref, jnp.maximum(s - n_qx, 0)))


def _in_proj(layer, xp, xs, g, w_in, tab_p, tab_s):
    m, ms = xp.shape[0], xs.shape[0]
    n_tab = tab_p[0].shape[0] // TM
    n_qkv, n_u, n_qx, n_g = QKV_W // TN, CONV_CH // TN, X_W // TN, N_BRANCH * D_MODEL // TN
    a0, qx0 = COL_A // TN, COL_QX // TN
    xspec = [pl.BlockSpec((TM, D_MODEL), lambda i, s: (i, 0)),
             pl.BlockSpec((ms, D_MODEL), lambda i, s: (0, 0)),
             pl.BlockSpec((1, D_MODEL), lambda i, s: (0, 0))]
    wspec = lambda fn: pl.BlockSpec((None, D_MODEL, TN), lambda i, s: (layer, 0, fn(s)))
    ptile = lambda fn: pl.BlockSpec((TM, TN), lambda i, s: (i, fn(s)))
    whole = lambda wd: pl.BlockSpec((ms, wd), lambda i, s: (0, 0))
    shp = lambda r, wd: jax.ShapeDtypeStruct((r, wd), F32)
    scratch = [pltpu.VMEM((TM, D_MODEL), BF16), pltpu.VMEM((ms, D_MODEL), BF16)]

    qkv_p, u_p, qkv_s, u_s = pl.pallas_call(
        _proj_qkvu_body,
        grid=(m // TM, n_qkv + n_u),
        in_specs=xspec + [
            wspec(lambda s: s),
            wspec(lambda s: jnp.clip(s + n_u, a0 + n_u, a0 + 2 * n_u - 1)),
            *[pl.BlockSpec((TM, HEAD_DIM), lambda i, s: (i % n_tab, 0)) for _ in range(3)],
            *[pl.BlockSpec((ms, HEAD_DIM), lambda i, s: (0, 0)) for _ in range(3)],
        ],
        out_specs=[ptile(lambda s: jnp.minimum(s, n_qkv - 1)), ptile(lambda s: jnp.maximum(s - n_qkv, 0)),
                   whole(QKV_W), whole(CONV_CH)],
        out_shape=[shp(m, QKV_W), shp(m, CONV_CH), shp(ms, QKV_W), shp(ms, CONV_CH)],
        scratch_shapes=scratch,
        compiler_params=_params("arbitrary", "arbitrary"),
        name="proj_qkvu",
    )(xp, xs, g, w_in, w_in, *tab_p, *tab_s)

    qx_p, sg_p, qx_s, sg_s = pl.pallas_call(
        _proj_qxg_body,
        grid=(m // TM, n_qx + n_g),
        in_specs=xspec + [wspec(lambda s: qx0 + s)],
        out_specs=[ptile(lambda s: jnp.minimum(s, n_qx - 1)), ptile(lambda s: jnp.maximum(s - n_qx, 0)),
                   whole(X_W), whole(N_BRANCH * D_MODEL)],
        out_shape=[shp(m, X_W), shp(m, N_BRANCH * D_MODEL), shp(ms, X_W), shp(ms, N_BRANCH * D_MODEL)],
        scratch_shapes=scratch,
        compiler_params=_params("arbitrary", "arbitrary"),
        name="proj_qxg",
    )(xp, xs, g, w_in)
    return (qkv_p, u_p, qx_p, sg_p), (qkv_s, u_s, qx_s, sg_s)


def _mem_proj_body(x_ref, g_ref, w_ref, o_ref, h_ref):
    @pl.when(pl.program_id(0) == 0)
    def _():
        h_ref[...] = _rms(x_ref[...], g_ref[...]).astype(BF16)

    o_ref[...] = _dot(h_ref[...], w_ref[...])


def _mem_proj(layer, x, g, w):
    m, n = x.shape[0], w.shape[-1]
    return pl.pallas_call(
        _mem_proj_body,
        grid=(n // TN,),
        in_specs=[pl.BlockSpec((m, D_MODEL), lambda j: (0, 0)),
                  pl.BlockSpec((1, D_MODEL), lambda j: (0, 0)),
                  pl.BlockSpec((None, D_MODEL, TN), lambda j: (layer, 0, j))],
        out_specs=pl.BlockSpec((m, TN), lambda j: (0, j)),
        out_shape=jax.ShapeDtypeStruct((m, n), F32),
        scratch_shapes=[pltpu.VMEM((m, D_MODEL), BF16)],
        compiler_params=_params("arbitrary"),
        name="mem_proj",
    )(x, g, w)


def _rope_tables(pos):
    half = ROT_DIM // 2
    inv = ROPE_THETA ** (-jnp.arange(half, dtype=F32) / half)
    ang = pos.astype(F32)[:, None] * inv[None, :]
    cos, sin = jnp.cos(ang), jnp.sin(ang)
    n = pos.shape[0]
    zeros = jnp.zeros((n, half), F32)
    rest0 = jnp.zeros((n, HEAD_DIM - ROT_DIM), F32)
    c = jnp.concatenate([cos, cos, jnp.ones((n, HEAD_DIM - ROT_DIM), F32)], axis=1)
    s1 = jnp.concatenate([-sin, zeros, rest0], axis=1)
    s2 = jnp.concatenate([zeros, sin, rest0], axis=1)
    return c, s1, s2


def _softmax_parts(s):
    m = jnp.max(s, axis=-1, keepdims=True)
    p = jnp.exp(s - m)
    return m, p, jnp.sum(p, axis=-1, keepdims=True)


def _lane_pack(cols, rows):
    lane = lax.broadcasted_iota(jnp.int32, (rows, LANES), 1)
    out = jnp.zeros((rows, LANES), F32)
    for h, c in enumerate(cols):
        out = jnp.where(lane == h, c, out)
    return out


def _attn_prompt_body(q_ref, k_ref, v_ref, o_ref, lse_ref, *, win, dil, tq):
    t0 = pl.program_id(1) * tq
    span = win + tq
    start = pl.multiple_of(jnp.maximum(t0 - win, 0), min(win, tq))
    qpos = t0 + lax.broadcasted_iota(jnp.int32, (tq, span), 0)
    kpos = start + lax.broadcasted_iota(jnp.int32, (tq, span), 1)
    dist = qpos - kpos
    mask = (dist >= 0) & (dist <= win) & ((dist & (dil - 1)) == 0)
    lses = []
    for h in range(HEADS_PER_GROUP):
        hs = slice(h * HEAD_DIM, (h + 1) * HEAD_DIM)
        q = (q_ref[:, hs] * (HEAD_DIM ** -0.5)).astype(BF16)
        k = k_ref[pl.ds(start, span), hs].astype(BF16)
        v = v_ref[pl.ds(start, span), hs].astype(BF16)
        s = jnp.where(mask, _dot_t(q, k), NEG)
        m, p, den = _softmax_parts(s)
        o_ref[:, hs] = _dot(p.astype(BF16), v) / den
        lses.append(m + jnp.log(den))
    lse_ref[...] = _lane_pack(lses, tq)


def _attn_prompt(qkv, gi):
    win, dil = GROUPS[gi]
    assert dil & (dil - 1) == 0 and win + ATT_TQ[gi] <= SEQ
    tq = ATT_TQ[gi]
    n_t = SEQ // tq
    kt, vt = ATT_W // ATT_OUT + gi, 2 * ATT_W // ATT_OUT + gi
    return pl.pallas_call(
        functools.partial(_attn_prompt_body, win=win, dil=dil, tq=tq),
        grid=(BATCH, n_t),
        in_specs=[
            pl.BlockSpec((tq, ATT_OUT), lambda b, t: (b * n_t + t, gi)),
            pl.BlockSpec((SEQ, ATT_OUT), lambda b, t: (b, kt), pipeline_mode=pl.Buffered(1)),
            pl.BlockSpec((SEQ, ATT_OUT), lambda b, t: (b, vt), pipeline_mode=pl.Buffered(1)),
        ],
        out_specs=[
            pl.BlockSpec((tq, ATT_OUT), lambda b, t: (b * n_t + t, 0)),
            pl.BlockSpec((tq, LANES), lambda b, t: (b * n_t + t, 0)),
        ],
        out_shape=[
            jax.ShapeDtypeStruct((BATCH * SEQ, ATT_OUT), F32),
            jax.ShapeDtypeStruct((BATCH * SEQ, LANES), F32),
        ],
        compiler_params=_params("parallel", "arbitrary"),
        name="attn_prompt_g%d" % gi,
    )(qkv, qkv, qkv)


def _ln_swish(c, g_ref, b_ref):
    mu = jnp.mean(c, axis=-1, keepdims=True)
    xc = c - mu
    y = xc * lax.rsqrt(jnp.mean(xc * xc, axis=-1, keepdims=True) + EPS) * g_ref[...] + b_ref[...]
    return y * jax.nn.sigmoid(y)


def _conv_prompt_body(u_ref, halo_ref, w_ref, b_ref, g_ref, bl_ref, o_ref, tail_ref, ext_ref, sh_ref, c_ref, *, n_t):
    t = pl.program_id(1)
    halo = halo_ref[...]
    ext_ref[0:CONV_HALO, :] = jnp.where(t > 0, halo, jnp.zeros_like(halo))
    ext_ref[CONV_HALO:, :] = u_ref[...]
    first = CONV_HALO - (CONV_WIDTH - 1)
    n_sh = CONV_HALO + CONV_ROWS - SUBLANES
    for c in range(CONV_CH // LANES):
        cs = slice(c * LANES, (c + 1) * LANES)
        for s in range(1, SUBLANES):
            sh_ref[s, 0:n_sh, :] = ext_ref[s:s + n_sh, cs]
        for r0 in range(0, CONV_ROWS, CONV_PASS):
            acc = jnp.zeros((CONV_PASS, LANES), F32)
            for k in range(CONV_WIDTH):
                s, q = (first + k) % SUBLANES, r0 + (first + k) // SUBLANES * SUBLANES
                rows = ext_ref[q:q + CONV_PASS, cs] if s == 0 else sh_ref[s, q:q + CONV_PASS, :]
                acc = acc + rows * w_ref[k:k + 1, cs]
            c_ref[r0:r0 + CONV_PASS, cs] = acc
    for r0 in range(0, CONV_ROWS, LN_ROWS):
        rs = slice(r0, r0 + LN_ROWS)
        o_ref[rs, :] = _ln_swish(c_ref[rs, :] + b_ref[...], g_ref, bl_ref).astype(BF16)

    @pl.when(t == n_t - 1)
    def _():
        tail_ref[...] = u_ref[CONV_ROWS - (CONV_WIDTH - 1):, :]


def _conv_prompt(u, w, b, g, bl):
    n_t = SEQ // CONV_ROWS
    hpb = CONV_ROWS // CONV_HALO
    vec = pl.BlockSpec((1, CONV_CH), lambda b_, t: (0, 0))
    return pl.pallas_call(
        functools.partial(_conv_prompt_body, n_t=n_t),
        grid=(BATCH, n_t),
        in_specs=[
            pl.BlockSpec((CONV_ROWS, CONV_CH), lambda b_, t: (b_ * n_t + t, 0)),
            pl.BlockSpec((CONV_HALO, CONV_CH), lambda b_, t: (jnp.maximum((b_ * n_t + t) * hpb - 1, 0), 0)),
            pl.BlockSpec((CONV_WIDTH, CONV_CH), lambda b_, t: (0, 0)),
            vec, vec, vec,
        ],
        out_specs=[
            pl.BlockSpec((CONV_ROWS, CONV_CH), lambda b_, t: (b_ * n_t + t, 0)),
            pl.BlockSpec((None, CONV_WIDTH - 1, CONV_CH), lambda b_, t: (b_, 0, 0)),
        ],
        out_shape=[
            jax.ShapeDtypeStruct((BATCH * SEQ, CONV_CH), BF16),
            jax.ShapeDtypeStruct((BATCH, CONV_WIDTH - 1, CONV_CH), F32),
        ],
        scratch_shapes=[
            pltpu.VMEM((CONV_HALO + CONV_ROWS, CONV_CH), F32),
            pltpu.VMEM((SUBLANES, CONV_HALO + CONV_ROWS, LANES), F32),
            pltpu.VMEM((CONV_ROWS, CONV_CH), F32),
        ],
        compiler_params=_params("parallel", "arbitrary"),
        name="conv_prompt",
    )(u, u, w, b, g, bl)


def _cross_prompt_body(q_ref, k_ref, v_ref, o_ref):
    for h in range(TN // X_HEAD_DIM):
        hs = slice(h * X_HEAD_DIM, (h + 1) * X_HEAD_DIM)
        s = _dot_t(q_ref[:, hs].astype(BF16), k_ref[:, hs].astype(BF16)) * (X_HEAD_DIM ** -0.5)
        _, p, den = _softmax_parts(s)
        o_ref[:, hs] = _dot(p.astype(BF16), v_ref[:, hs].astype(BF16)) / den


def _cross_prompt(qx, mkv, *, tq):
    n_t = SEQ // tq
    n_hp = X_W // TN
    mkv3 = mkv.reshape(BATCH, N_MEM, 2 * X_W)
    return pl.pallas_call(
        _cross_prompt_body,
        grid=(BATCH, n_t, n_hp),
        in_specs=[
            pl.BlockSpec((tq, TN), lambda b, t, hp: (b * n_t + t, hp)),
            pl.BlockSpec((None, N_MEM, TN), lambda b, t, hp: (b, 0, hp)),
            pl.BlockSpec((None, N_MEM, TN), lambda b, t, hp: (b, 0, n_hp + hp)),
        ],
        out_specs=pl.BlockSpec((tq, TN), lambda b, t, hp: (b * n_t + t, hp)),
        out_shape=jax.ShapeDtypeStruct((BATCH * SEQ, X_W), F32),
        compiler_params=_params("parallel", "parallel", "arbitrary"),
        name="cross_prompt",
    )(qx, mkv3, mkv3)


def _post_body(c_ref, o0_ref, o1_ref, o2_ref, l0_ref, l1_ref, l2_ref, om_ref, ga_ref, gb_ref, gc_ref,
               wc_ref, wa_ref, wx_ref, wo_ref, x_ref, out_ref, oatt_ref, omem_ref, merged_ref, *, n_n):
    n = pl.program_id(1)

    @pl.when(n == 0)
    def _():
        for h in range(HEADS_PER_GROUP):
            hs = slice(h * HEAD_DIM, (h + 1) * HEAD_DIM)
            l0, l1, l2 = l0_ref[:, h:h + 1], l1_ref[:, h:h + 1], l2_ref[:, h:h + 1]
            m = jnp.maximum(jnp.maximum(l0, l1), l2)
            e0, e1, e2 = jnp.exp(l0 - m), jnp.exp(l1 - m), jnp.exp(l2 - m)
            mix = (e0 * o0_ref[:, hs] + e1 * o1_ref[:, hs] + e2 * o2_ref[:, hs]) / (e0 + e1 + e2)
            oatt_ref[:, hs] = mix.astype(BF16)
        omem_ref[...] = om_ref[...].astype(BF16)

    @pl.when(n < n_n)
    def _():
        merged = (ga_ref[...] * _dot(c_ref[...], wc_ref[...])
                  + gb_ref[...] * _dot(oatt_ref[...], wa_ref[...])
                  + gc_ref[...] * _dot(omem_ref[...], wx_ref[...]))
        merged_ref[:, pl.ds(pl.multiple_of(n * TN, TN), TN)] = merged.astype(BF16)

    @pl.when(n >= n_n)
    def _():
        out_ref[...] = x_ref[...] + _dot(merged_ref[...], wo_ref[...])


def _post(c_act, o_g, lse_g, o_mem, sg, wc, wa, wx, wo, x, *, tm):
    m = x.shape[0]
    n_n = D_MODEL // TN
    first = lambda n: jnp.minimum(n, n_n - 1)
    second = lambda n: jnp.maximum(n - n_n, 0)
    row = lambda w: pl.BlockSpec((tm, w), lambda i, n: (i, 0))
    gate = lambda br: pl.BlockSpec((tm, TN), lambda i, n: (i, br * n_n + first(n)))
    wcol = lambda k: pl.BlockSpec((k, TN), lambda i, n: (0, first(n)))
    otile = pl.BlockSpec((tm, TN), lambda i, n: (i, second(n)))
    return pl.pallas_call(
        functools.partial(_post_body, n_n=n_n),
        grid=(m // tm, 2 * n_n),
        in_specs=[
            row(CONV_CH), row(ATT_OUT), row(ATT_OUT), row(ATT_OUT), row(LANES), row(LANES), row(LANES), row(X_W),
            gate(0), gate(1), gate(2),
            wcol(CONV_CH), wcol(ATT_OUT), wcol(X_W),
            pl.BlockSpec((D_MODEL, TN), lambda i, n: (0, second(n))),
            otile,
        ],
        out_specs=otile,
        out_shape=jax.ShapeDtypeStruct((m, D_MODEL), F32),
        scratch_shapes=[
            pltpu.VMEM((tm, ATT_OUT), BF16),
            pltpu.VMEM((tm, X_W), BF16),
            pltpu.VMEM((tm, D_MODEL), BF16),
        ],
        compiler_params=_params("parallel", "arbitrary"),
        name="post",
    )(c_act, *o_g, *lse_g, o_mem, sg, sg, sg, wc, wa, wx, wo, x)


def _mix_sample_body(qkv_ref, qx_ref, u_ref, w0_ref, w1_ref, w2_ref, cs_ref, mem_ref, wdw_ref, b_ref, g_ref, bl_ref,
                     o0_ref, o1_ref, o2_ref, l0_ref, l1_ref, l2_ref, c_ref, om_ref, ncs_ref):
    scale = HEAD_DIM ** -0.5
    for gi, (win_ref, o_ref, l_ref) in enumerate(((w0_ref, o0_ref, l0_ref), (w1_ref, o1_ref, l1_ref),
                                                  (w2_ref, o2_ref, l2_ref))):
        lses = []
        for h in range(HEADS_PER_GROUP):
            col = gi * ATT_OUT + h * HEAD_DIM
            q = qkv_ref[:, col:col + HEAD_DIM]
            k_new = qkv_ref[:, ATT_W + col:ATT_W + col + HEAD_DIM]
            v_new = qkv_ref[:, 2 * ATT_W + col:2 * ATT_W + col + HEAD_DIM]
            k_buf, v_buf = win_ref[:, 0, h, :], win_ref[:, 1, h, :]
            s_buf = jnp.sum(k_buf * q, axis=-1, keepdims=True) * scale
            s_new = jnp.sum(k_new * q, axis=-1, keepdims=True) * scale
            m = jnp.maximum(jnp.max(s_buf, axis=0, keepdims=True), s_new)
            p_buf, p_new = jnp.exp(s_buf - m), jnp.exp(s_new - m)
            den = jnp.sum(p_buf, axis=0, keepdims=True) + p_new
            o = (jnp.sum(p_buf * v_buf, axis=0, keepdims=True) + p_new * v_new) / den
            o_ref[:, h * HEAD_DIM:(h + 1) * HEAD_DIM] = o
            lses.append(m + jnp.log(den))
        l_ref[...] = _lane_pack(lses, 1)
    for h in range(X_HEADS):
        hs = slice(h * X_HEAD_DIM, (h + 1) * X_HEAD_DIM)
        s = jnp.sum(mem_ref[:, 0, h, :] * qx_ref[:, hs], axis=-1, keepdims=True) * (X_HEAD_DIM ** -0.5)
        p = jnp.exp(s - jnp.max(s, axis=0, keepdims=True))
        om_ref[:, hs] = jnp.sum(p * mem_ref[:, 1, h, :], axis=0, keepdims=True) / jnp.sum(p, axis=0, keepdims=True)
    hist = CONV_WIDTH - 1
    c = (jnp.sum(cs_ref[...] * wdw_ref[0:hist, :], axis=0, keepdims=True)
         + u_ref[...] * wdw_ref[hist:hist + 1, :] + b_ref[...])
    c_ref[...] = _ln_swish(c, g_ref, bl_ref)
    ncs_ref[0:hist - 1, :] = cs_ref[1:hist, :]
    ncs_ref[hist - 1:hist, :] = u_ref[...]


def _mix_sample(layer, qkv, qx, u, wins, conv_state, mem_kv, w, b, g, bl):
    row = lambda wd: pl.BlockSpec((None, 1, wd), lambda s: (s, 0, 0))
    vec = pl.BlockSpec((1, CONV_CH), lambda s: (0, 0))
    hist = CONV_WIDTH - 1
    win_specs, win_args = [], []
    for (win, dil), arr in zip(GROUPS, wins):
        win_args.append(arr.reshape(DEPTH, DEC_BATCH, win // dil, dil, 2, HEADS_PER_GROUP, HEAD_DIM))
        win_specs.append(pl.BlockSpec((None, None, win // dil, None, 2, HEADS_PER_GROUP, HEAD_DIM),
                                      lambda s: (layer, s, 0, 0, 0, 0, 0)))
    f32_row = lambda wd: jax.ShapeDtypeStruct((DEC_BATCH, 1, wd), F32)
    outs = pl.pallas_call(
        _mix_sample_body,
        grid=(DEC_BATCH,),
        in_specs=[row(QKV_W), row(X_W), row(CONV_CH), *win_specs,
                  pl.BlockSpec((None, None, hist, CONV_CH), lambda s: (layer, s, 0, 0)),
                  pl.BlockSpec((None, None, N_MEM, 2, X_HEADS, X_HEAD_DIM), lambda s: (layer, s, 0, 0, 0, 0)),
                  pl.BlockSpec((CONV_WIDTH, CONV_CH), lambda s: (0, 0)), vec, vec, vec],
        out_specs=[row(ATT_OUT)] * 3 + [row(LANES)] * 3 + [row(CONV_CH), row(X_W),
                                                            pl.BlockSpec((None, hist, CONV_CH), lambda s: (s, 0, 0))],
        out_shape=[f32_row(ATT_OUT)] * 3 + [f32_row(LANES)] * 3 + [
            f32_row(CONV_CH), f32_row(X_W), jax.ShapeDtypeStruct((DEC_BATCH, hist, CONV_CH), F32)],
        compiler_params=_params("parallel"),
        name="mix_sample",
    )(qkv.reshape(DEC_BATCH, 1, QKV_W), qx.reshape(DEC_BATCH, 1, X_W), u.reshape(DEC_BATCH, 1, CONV_CH),
      *win_args, conv_state, mem_kv, w, b, g, bl)
    flat = [o.reshape(DEC_BATCH, -1) for o in outs[:8]]
    return flat[0:3], flat[3:6], flat[6].astype(BF16), flat[7], outs[8]


def _shift_window_body(x_ref, nxt_ref, new_ref, o_ref, *, n_c):
    rows = x_ref.shape[0]
    o_ref[0:rows - 1] = x_ref[1:rows]
    last = pl.program_id(2) == n_c - 1
    o_ref[rows - 1:rows] = jnp.where(last, new_ref[...], nxt_ref[...])


def _shift_window(state, new):
    rows = state.shape[2]
    rc = min(rows, SHIFT_ROWS)
    n_c = rows // rc
    tail = state.shape[3:]
    zeros = (0,) * len(tail)
    one = pl.BlockSpec((None, None, 1) + tail, lambda l, b, c: (l, b, jnp.minimum((c + 1) * rc, rows - 1)) + zeros)
    return pl.pallas_call(
        functools.partial(_shift_window_body, n_c=n_c),
        grid=(DEPTH, DEC_BATCH, n_c),
        in_specs=[pl.BlockSpec((None, None, rc) + tail, lambda l, b, c: (l, b, c) + zeros),
                  one,
                  pl.BlockSpec((None, None, 1) + tail, lambda l, b, c: (l, b, 0) + zeros)],
        out_specs=pl.BlockSpec((None, None, rc) + tail, lambda l, b, c: (l, b, c) + zeros),
        out_shape=jax.ShapeDtypeStruct(state.shape, state.dtype),
        compiler_params=_params("parallel", "parallel", "arbitrary"),
        name="shift_window",
    )(state, state, new)


def kernel(x_prompt, x_sample, state_win0, state_win1, state_win2, state_conv, cache_mem_kv, mem_prompt, w_ffn1_norm, w_ffn1_gate, w_ffn1_up, w_ffn1_down, w_mix_norm, w_in, w_dw, b_dw, g_cln, b_cln, w_conv_out, w_att_out, w_x_out, w_o, w_mem_norm, w_mem_kv, w_ffn2_norm, w_ffn2_gate, w_ffn2_up, w_ffn2_down, w_final_norm):
    state_wins = (state_win0, state_win1, state_win2)
    hp = x_prompt.reshape(BATCH * SEQ, D_MODEL)
    hs = x_sample.reshape(DEC_BATCH, D_MODEL)
    mem = mem_prompt.reshape(BATCH * N_MEM, D_MODEL)
    tab_p = _rope_tables(jnp.arange(SEQ))
    tab_s = _rope_tables(jnp.full((DEC_BATCH,), PAST_LEN))
    vec = lambda a: a.reshape(1, -1)
    g_final = vec(w_final_norm)
    ffn1_w = tuple(a.astype(BF16) for a in (w_ffn1_gate, w_ffn1_up, w_ffn1_down))
    ffn2_w = tuple(a.astype(BF16) for a in (w_ffn2_gate, w_ffn2_up, w_ffn2_down))
    w_in_bf, w_mem_kv_bf = w_in.astype(BF16), w_mem_kv.astype(BF16)

    win_p = [[] for _ in GROUPS]
    kv_s = [[] for _ in GROUPS]
    conv_p, conv_s, mem_p = [], [], []
    for l in range(DEPTH):
        last = l == DEPTH - 1
        conv_w = (w_dw[l], vec(b_dw[l]), vec(g_cln[l]), vec(b_cln[l]))
        out_w = tuple(a[l].astype(BF16) for a in (w_conv_out, w_att_out, w_x_out, w_o))

        mkv = _mem_proj(l, mem, vec(w_mem_norm[l]), w_mem_kv_bf)
        mem_p.append(mkv.reshape(BATCH, N_MEM, 2, X_HEADS, X_HEAD_DIM))

        hp, hs = _ffn(l, hp, hs, vec(w_ffn1_norm[l]), *ffn1_w)
        (qkv, u, qx, sg), (qkv_s, u_s, qx_s, sg_s) = _in_proj(l, hp, hs, vec(w_mix_norm[l]), w_in_bf, tab_p, tab_s)

        att = [_attn_prompt(qkv, gi) for gi in range(N_GROUPS)]
        c_act, conv_tail = _conv_prompt(u, *conv_w)
        o_mem = _cross_prompt(qx, mkv, tq=1024)
        hp = _post(c_act, [a[0] for a in att], [a[1] for a in att], o_mem, sg, *out_w, hp, tm=512)
        qkv3 = qkv.reshape(BATCH, SEQ, QKV_W)
        for gi, (win, _) in enumerate(GROUPS):
            keep = min(win, SEQ)
            cut = lambda c0: qkv3[:, SEQ - keep:, c0 + gi * ATT_OUT:c0 + (gi + 1) * ATT_OUT].reshape(
                BATCH, keep, HEADS_PER_GROUP, HEAD_DIM)
            win_p[gi].append(jnp.stack([cut(ATT_W), cut(2 * ATT_W)], axis=2))
        conv_p.append(conv_tail)

        o_g, lse_g, c_act, o_mem, conv_new = _mix_sample(l, qkv_s, qx_s, u_s, state_wins, state_conv, cache_mem_kv,
                                                         *conv_w)
        hs = _post(c_act, o_g, lse_g, o_mem, sg_s, *out_w, hs, tm=DEC_BATCH)
        for gi in range(N_GROUPS):
            cut = lambda c0: qkv_s[:, c0 + gi * ATT_OUT:c0 + (gi + 1) * ATT_OUT].reshape(
                DEC_BATCH, HEADS_PER_GROUP, HEAD_DIM)
            kv_s[gi].append(jnp.stack([cut(ATT_W), cut(2 * ATT_W)], axis=1))
        conv_s.append(conv_new)

        hp, hs = _ffn(l, hp, hs, vec(w_ffn2_norm[l]), *ffn2_w, g_final if last else None)

    st = lambda xs: jnp.stack(xs, axis=0)
    win_s = [_shift_window(s, st(k)[:, :, None]) for s, k in zip(state_wins, kv_s)]
    return (hp.reshape(BATCH, SEQ, D_MODEL), hs.reshape(DEC_BATCH, 1, D_MODEL),
            st(win_p[0]), st(win_p[1]), st(win_p[2]), st(conv_p), st(mem_p),
            win_s[0], win_s[1], win_s[2], st(conv_s))
```

```python
import functools

import jax
import jax.numpy as jnp
from jax import lax
from jax.experimental import pallas as pl
from jax.experimental.pallas import tpu as pltpu

D_MODEL = 2048
BATCH = 2
SEQ = 4096
DEPTH = 2
DEC_BATCH = 32
PAST_LEN = 8192
HEAD_DIM = 128
HEADS_PER_GROUP = 4
GROUPS = ((128, 1), (512, 4), (2048, 16))
N_GROUPS = len(GROUPS)
ATT_W = N_GROUPS * HEADS_PER_GROUP * HEAD_DIM
ATT_OUT = HEADS_PER_GROUP * HEAD_DIM
BLOCK = 128
ROT_DIM = HEAD_DIM // 4
ROPE_THETA = 500000.0
CONV_CH = 3 * D_MODEL // 4
CONV_WIDTH = 31
X_HEADS = 4
X_HEAD_DIM = D_MODEL // 8
X_W = X_HEADS * X_HEAD_DIM
N_MEM = 256
D_FF = 256 * ((8 * D_MODEL // 3 + 255) // 256)
N_BRANCH = 3
QKV_W = 3 * ATT_W
COL_A = QKV_W
COL_QX = COL_A + 2 * CONV_CH
COL_GATES = COL_QX + X_W
IN_W = COL_GATES + N_BRANCH * D_MODEL
EPS = 1e-6
NEG = -1e30

F32 = jnp.float32
BF16 = jnp.bfloat16

LANES = 128
SUBLANES = 8
VMEM_LIMIT_BYTES = 58 * 1024 * 1024
TN = 512
TF = 512
TF_FINAL = 256
TM = 1024
EDGE_ROWS = 256
CONV_ROWS = 128
CONV_HALO = 32
CONV_PASS = 64
LN_ROWS = 16
ATT_TQ = (256, 512, 512)
SHIFT_ROWS = 512


def _params(*sem):
    return pltpu.CompilerParams(dimension_semantics=sem, vmem_limit_bytes=VMEM_LIMIT_BYTES)


def _rms(x, g):
    return x * lax.rsqrt(jnp.mean(x * x, axis=-1, keepdims=True) + EPS) * g


def _row_chunks(n):
    step = min(n, EDGE_ROWS)
    return [slice(r, r + step) for r in range(0, n, step)]


def _norm_rows(x_ref, g_ref, h_ref):
    for r in _row_chunks(x_ref.shape[0]):
        h_ref[r, :] = _rms(x_ref[r, :], g_ref[...]).astype(BF16)


def _dot(a, b):
    return jnp.dot(a, b, preferred_element_type=F32)


def _dot_t(a, b):
    return lax.dot_general(a, b, (((1,), (1,)), ((), ())), preferred_element_type=F32)


def _ffn_body(xp_ref, xs_ref, g_ref, wg_ref, wu_ref, wd_ref, *rest, n_f, final_norm):
    if final_norm:
        gf_ref, op_ref, os_ref, hp_ref, hs_ref = rest
    else:
        op_ref, os_ref, hp_ref, hs_ref = rest
    i, f = pl.program_id(0), pl.program_id(1)
    wg, wu, wd = wg_ref[...], wu_ref[...], wd_ref

    def rows(x_ref, o_ref, h_ref):
        @pl.when(f == 0)
        def _():
            _norm_rows(x_ref, g_ref, h_ref)
            o_ref[...] = jnp.zeros_like(o_ref)

        h = h_ref[...]
        gate = _dot(h, wg)
        act = (gate * jax.nn.sigmoid(gate) * _dot(h, wu)).astype(BF16)
        for n in range(0, D_MODEL, TN):
            o_ref[:, n:n + TN] += _dot(act, wd[:, n:n + TN])

        @pl.when(f == n_f - 1)
        def _():
            for r in _row_chunks(x_ref.shape[0]):
                y = x_ref[r, :] + 0.5 * o_ref[r, :]
                if final_norm:
                    y = _rms(y, gf_ref[...])
                o_ref[r, :] = y

    rows(xp_ref, op_ref, hp_ref)

    @pl.when(i == 0)
    def _():
        rows(xs_ref, os_ref, hs_ref)


def _ffn(layer, xp, xs, g, wg, wu, wd, gf=None):
    m, ms = xp.shape[0], xs.shape[0]
    final_norm = gf is not None
    tf = TF_FINAL if final_norm else TF
    n_f = D_FF // tf
    vec = pl.BlockSpec((1, D_MODEL), lambda i, f: (0, 0))
    in_specs = [
        pl.BlockSpec((TM, D_MODEL), lambda i, f: (i, 0), pipeline_mode=pl.Buffered(1)),
        pl.BlockSpec((ms, D_MODEL), lambda i, f: (0, 0)),
        vec,
        pl.BlockSpec((None, D_MODEL, tf), lambda i, f: (layer, 0, f)),
        pl.BlockSpec((None, D_MODEL, tf), lambda i, f: (layer, 0, f)),
        pl.BlockSpec((None, tf, D_MODEL), lambda i, f: (layer, f, 0)),
    ]
    args = [xp, xs, g, wg, wu, wd]
    if final_norm:
        in_specs.append(vec)
        args.append(gf)
    return pl.pallas_call(
        functools.partial(_ffn_body, n_f=n_f, final_norm=final_norm),
        grid=(m // TM, n_f),
        in_specs=in_specs,
        out_specs=[pl.BlockSpec((TM, D_MODEL), lambda i, f: (i, 0)),
                   pl.BlockSpec((ms, D_MODEL), lambda i, f: (0, 0))],
        out_shape=[jax.ShapeDtypeStruct((m, D_MODEL), F32), jax.ShapeDtypeStruct((ms, D_MODEL), F32)],
        scratch_shapes=[pltpu.VMEM((TM, D_MODEL), BF16), pltpu.VMEM((ms, D_MODEL), BF16)],
        compiler_params=_params("arbitrary", "arbitrary"),
        name="ffn",
    )(*args)


def _rope(z, c_ref, s1_ref, s2_ref):
    half = ROT_DIM // 2
    c, s1, s2 = c_ref[...], s1_ref[...], s2_ref[...]
    heads = []
    for h in range(z.shape[-1] // HEAD_DIM):
        zh = z[:, h * HEAD_DIM:(h + 1) * HEAD_DIM]
        heads.append(zh * c + pltpu.roll(zh, HEAD_DIM - half, 1) * s1 + pltpu.roll(zh, half, 1) * s2)
    return jnp.concatenate(heads, axis=1)


def _col_tile(ref, t):
    return ref.at[:, pl.ds(pl.multiple_of(t * TN, TN), TN)]


def _proj_qkvu_body(xp_ref, xs_ref, g_ref, w_ref, w2_ref, cp_ref, s1p_ref, s2p_ref, cs_ref, s1s_ref, s2s_ref,
                    qkvp_ref, up_ref, qkvs_ref, us_ref, hp_ref, hs_ref):
    i, s = pl.program_id(0), pl.program_id(1)
    n_rot, n_qkv = 2 * ATT_W // TN, QKV_W // TN
    w = w_ref[...]

    def rows(x_ref, h_ref, tabs, qkv_ref, u_ref):
        @pl.when(s == 0)
        def _():
            _norm_rows(x_ref, g_ref, h_ref)

        h = h_ref[...]
        z = _dot(h, w)

        @pl.when(s < n_rot)
        def _():
            qkv_ref[...] = _rope(z, *tabs)

        @pl.when((s >= n_rot) & (s < n_qkv))
        def _():
            qkv_ref[...] = z

        @pl.when(s >= n_qkv)
        def _():
            u_ref[...] = z * jax.nn.sigmoid(_dot(h, w2_ref[...]))

    rows(xp_ref, hp_ref, (cp_ref, s1p_ref, s2p_ref), qkvp_ref, up_ref)

    @pl.when(i == 0)
    def _():
        rows(xs_ref, hs_ref, (cs_ref, s1s_ref, s2s_ref),
             _col_tile(qkvs_ref, jnp.minimum(s, n_qkv - 1)), _col_tile(us_ref, jnp.maximum(s - n_qkv, 0)))


def _proj_qxg_body(xp_ref, xs_ref, g_ref, w_ref, qxp_ref, sgp_ref, qxs_ref, sgs_ref, hp_ref, hs_ref):
    i, s = pl.program_id(0), pl.program_id(1)
    n_qx = X_W // TN
    w = w_ref[...]

    def rows(x_ref, h_ref, qx_ref, sg_ref):
        @pl.when(s == 0)
        def _():
            _norm_rows(x_ref, g_ref, h_ref)

        z = _dot(h_ref[...], w)

        @pl.when(s < n_qx)
        def _():
            qx_ref[...] = z

        @pl.when(s >= n_qx)
        def _():
            sg_ref[...] = jax.nn.sigmoid(z).astype(BF16)

    rows(xp_ref, hp_ref, qxp_ref, sgp_ref)

    @pl.when(i == 0)
    def _():
        rows(xs_ref, hs_ref, _col_tile(qxs_ref, jnp.minimum(s, n_qx - 1)), _col_tile(sgs_ref, jnp.maximum(s - n_qx, 0)))


def _in_proj(layer, xp, xs, g, w_in, tab_p, tab_s):
    m, ms = xp.shape[0], xs.shape[0]
    n_tab = tab_p[0].shape[0] // TM
    n_qkv, n_u, n_qx, n_g = QKV_W // TN, CONV_CH // TN, X_W // TN, N_BRANCH * D_MODEL // TN
    a0, qx0 = COL_A // TN, COL_QX // TN
    xspec = [pl.BlockSpec((TM, D_MODEL), lambda i, s: (i, 0)),
             pl.BlockSpec((ms, D_MODEL), lambda i, s: (0, 0)),
             pl.BlockSpec((1, D_MODEL), lambda i, s: (0, 0))]
    wspec = lambda fn: pl.BlockSpec((None, D_MODEL, TN), lambda i, s: (layer, 0, fn(s)))
    ptile = lambda fn: pl.BlockSpec((TM, TN), lambda i, s: (i, fn(s)))
    whole = lambda wd: pl.BlockSpec((ms, wd), lambda i, s: (0, 0))
    shp = lambda r, wd: jax.ShapeDtypeStruct((r, wd), F32)
    scratch = [pltpu.VMEM((TM, D_MODEL), BF16), pltpu.VMEM((ms, D_MODEL), BF16)]

    qkv_p, u_p, qkv_s, u_s = pl.pallas_call(
        _proj_qkvu_body,
        grid=(m // TM, n_qkv + n_u),
        in_specs=xspec + [
            wspec(lambda s: s),
            wspec(lambda s: jnp.clip(s + n_u, a0 + n_u, a0 + 2 * n_u - 1)),
            *[pl.BlockSpec((TM, HEAD_DIM), lambda i, s: (i % n_tab, 0)) for _ in range(3)],
            *[pl.BlockSpec((ms, HEAD_DIM), lambda i, s: (0, 0)) for _ in range(3)],
        ],
        out_specs=[ptile(lambda s: jnp.minimum(s, n_qkv - 1)), ptile(lambda s: jnp.maximum(s - n_qkv, 0)),
                   whole(QKV_W), whole(CONV_CH)],
        out_shape=[shp(m, QKV_W), shp(m, CONV_CH), shp(ms, QKV_W), shp(ms, CONV_CH)],
        scratch_shapes=scratch,
        compiler_params=_params("arbitrary", "arbitrary"),
        name="proj_qkvu",
    )(xp, xs, g, w_in, w_in, *tab_p, *tab_s)

    qx_p, sg_p, qx_s, sg_s = pl.pallas_call(
        _proj_qxg_body,
        grid=(m // TM, n_qx + n_g),
        in_specs=xspec + [wspec(lambda s: qx0 + s)],
        out_specs=[ptile(lambda s: jnp.minimum(s, n_qx - 1)), ptile(lambda s: jnp.maximum(s - n_qx, 0)),
                   whole(X_W), whole(N_BRANCH * D_MODEL)],
        out_shape=[shp(m, X_W), jax.ShapeDtypeStruct((m, N_BRANCH * D_MODEL), BF16),
                   shp(ms, X_W), jax.ShapeDtypeStruct((ms, N_BRANCH * D_MODEL), BF16)],
        scratch_shapes=scratch,
        compiler_params=_params("arbitrary", "arbitrary"),
        name="proj_qxg",
    )(xp, xs, g, w_in)
    return (qkv_p, u_p, qx_p, sg_p), (qkv_s, u_s, qx_s, sg_s)


def _mem_proj_body(x_ref, g_ref, w_ref, o_ref, h_ref):
    @pl.when(pl.program_id(0) == 0)
    def _():
        h_ref[...] = _rms(x_ref[...], g_ref[...]).astype(BF16)

    o_ref[...] = _dot(h_ref[...], w_ref[...])


def _mem_proj(layer, x, g, w):
    m, n = x.shape[0], w.shape[-1]
    return pl.pallas_call(
        _mem_proj_body,
        grid=(n // TN,),
        in_specs=[pl.BlockSpec((m, D_MODEL), lambda j: (0, 0)),
                  pl.BlockSpec((1, D_MODEL), lambda j: (0, 0)),
                  pl.BlockSpec((None, D_MODEL, TN), lambda j: (layer, 0, j))],
        out_specs=pl.BlockSpec((m, TN), lambda j: (0, j)),
        out_shape=jax.ShapeDtypeStruct((m, n), F32),
        scratch_shapes=[pltpu.VMEM((m, D_MODEL), BF16)],
        compiler_params=_params("arbitrary"),
        name="mem_proj",
    )(x, g, w)


def _rope_tables(pos):
    half = ROT_DIM // 2
    inv = ROPE_THETA ** (-jnp.arange(half, dtype=F32) / half)
    ang = pos.astype(F32)[:, None] * inv[None, :]
    cos, sin = jnp.cos(ang), jnp.sin(ang)
    n = pos.shape[0]
    zeros = jnp.zeros((n, half), F32)
    rest0 = jnp.zeros((n, HEAD_DIM - ROT_DIM), F32)
    c = jnp.concatenate([cos, cos, jnp.ones((n, HEAD_DIM - ROT_DIM), F32)], axis=1)
    s1 = jnp.concatenate([-sin, zeros, rest0], axis=1)
    s2 = jnp.concatenate([zeros, sin, rest0], axis=1)
    return c, s1, s2


def _softmax_parts(s):
    m = jnp.max(s, axis=-1, keepdims=True)
    p = jnp.exp(s - m)
    return m, p, jnp.sum(p, axis=-1, keepdims=True)


def _lane_pack(cols, rows):
    lane = lax.broadcasted_iota(jnp.int32, (rows, LANES), 1)
    out = jnp.zeros((rows, LANES), F32)
    for h, c in enumerate(cols):
        out = jnp.where(lane == h, c, out)
    return out


def _attn_prompt_body(q_ref, k_ref, v_ref, o_ref, lse_ref, *, win, dil, tq):
    t0 = pl.program_id(1) * tq
    span = win + tq
    start = pl.multiple_of(jnp.maximum(t0 - win, 0), min(win, tq))
    qpos = t0 + lax.broadcasted_iota(jnp.int32, (tq, span), 0)
    kpos = start + lax.broadcasted_iota(jnp.int32, (tq, span), 1)
    dist = qpos - kpos
    mask = (dist >= 0) & (dist <= win) & ((dist & (dil - 1)) == 0)
    lses = []
    for h in range(HEADS_PER_GROUP):
        hs = slice(h * HEAD_DIM, (h + 1) * HEAD_DIM)
        q = (q_ref[:, hs] * (HEAD_DIM ** -0.5)).astype(BF16)
        k = k_ref[pl.ds(start, span), hs].astype(BF16)
        v = v_ref[pl.ds(start, span), hs].astype(BF16)
        s = jnp.where(mask, _dot_t(q, k), NEG)
        m, p, den = _softmax_parts(s)
        o_ref[:, hs] = _dot(p.astype(BF16), v) / den
        lses.append(m + jnp.log(den))
    lse_ref[...] = _lane_pack(lses, tq)


def _attn_prompt(qkv, gi):
    win, dil = GROUPS[gi]
    assert dil & (dil - 1) == 0 and win + ATT_TQ[gi] <= SEQ
    tq = ATT_TQ[gi]
    n_t = SEQ // tq
    kt, vt = ATT_W // ATT_OUT + gi, 2 * ATT_W // ATT_OUT + gi
    return pl.pallas_call(
        functools.partial(_attn_prompt_body, win=win, dil=dil, tq=tq),
        grid=(BATCH, n_t),
        in_specs=[
            pl.BlockSpec((tq, ATT_OUT), lambda b, t: (b * n_t + t, gi)),
            pl.BlockSpec((SEQ, ATT_OUT), lambda b, t: (b, kt), pipeline_mode=pl.Buffered(1)),
            pl.BlockSpec((SEQ, ATT_OUT), lambda b, t: (b, vt), pipeline_mode=pl.Buffered(1)),
        ],
        out_specs=[
            pl.BlockSpec((tq, ATT_OUT), lambda b, t: (b * n_t + t, 0)),
            pl.BlockSpec((tq, LANES), lambda b, t: (b * n_t + t, 0)),
        ],
        out_shape=[
            jax.ShapeDtypeStruct((BATCH * SEQ, ATT_OUT), F32),
            jax.ShapeDtypeStruct((BATCH * SEQ, LANES), F32),
        ],
        compiler_params=_params("parallel", "arbitrary"),
        name="attn_prompt_g%d" % gi,
    )(qkv, qkv, qkv)


def _ln_swish(c, g_ref, b_ref):
    mu = jnp.mean(c, axis=-1, keepdims=True)
    xc = c - mu
    y = xc * lax.rsqrt(jnp.mean(xc * xc, axis=-1, keepdims=True) + EPS) * g_ref[...] + b_ref[...]
    return y * jax.nn.sigmoid(y)


def _conv_prompt_body(u_ref, halo_ref, w_ref, b_ref, g_ref, bl_ref, o_ref, tail_ref, ext_ref, sh_ref, c_ref, *, n_t):
    t = pl.program_id(1)
    halo = halo_ref[...]
    ext_ref[0:CONV_HALO, :] = jnp.where(t > 0, halo, jnp.zeros_like(halo))
    ext_ref[CONV_HALO:, :] = u_ref[...]
    first = CONV_HALO - (CONV_WIDTH - 1)
    n_sh = CONV_HALO + CONV_ROWS - SUBLANES
    for c in range(CONV_CH // LANES):
        cs = slice(c * LANES, (c + 1) * LANES)
        for s in range(1, SUBLANES):
            sh_ref[s, 0:n_sh, :] = ext_ref[s:s + n_sh, cs]
        for r0 in range(0, CONV_ROWS, CONV_PASS):
            acc = jnp.zeros((CONV_PASS, LANES), F32)
            for k in range(CONV_WIDTH):
                s, q = (first + k) % SUBLANES, r0 + (first + k) // SUBLANES * SUBLANES
                rows = ext_ref[q:q + CONV_PASS, cs] if s == 0 else sh_ref[s, q:q + CONV_PASS, :]
                acc = acc + rows * w_ref[k:k + 1, cs]
            c_ref[r0:r0 + CONV_PASS, cs] = acc
    for r0 in range(0, CONV_ROWS, LN_ROWS):
        rs = slice(r0, r0 + LN_ROWS)
        o_ref[rs, :] = _ln_swish(c_ref[rs, :] + b_ref[...], g_ref, bl_ref).astype(BF16)

    @pl.when(t == n_t - 1)
    def _():
        tail_ref[...] = u_ref[CONV_ROWS - (CONV_WIDTH - 1):, :]


def _conv_prompt(u, w, b, g, bl):
    n_t = SEQ // CONV_ROWS
    hpb = CONV_ROWS // CONV_HALO
    vec = pl.BlockSpec((1, CONV_CH), lambda b_, t: (0, 0))
    return pl.pallas_call(
        functools.partial(_conv_prompt_body, n_t=n_t),
        grid=(BATCH, n_t),
        in_specs=[
            pl.BlockSpec((CONV_ROWS, CONV_CH), lambda b_, t: (b_ * n_t + t, 0)),
            pl.BlockSpec((CONV_HALO, CONV_CH), lambda b_, t: (jnp.maximum((b_ * n_t + t) * hpb - 1, 0), 0)),
            pl.BlockSpec((CONV_WIDTH, CONV_CH), lambda b_, t: (0, 0)),
            vec, vec, vec,
        ],
        out_specs=[
            pl.BlockSpec((CONV_ROWS, CONV_CH), lambda b_, t: (b_ * n_t + t, 0)),
            pl.BlockSpec((None, CONV_WIDTH - 1, CONV_CH), lambda b_, t: (b_, 0, 0)),
        ],
        out_shape=[
            jax.ShapeDtypeStruct((BATCH * SEQ, CONV_CH), BF16),
            jax.ShapeDtypeStruct((BATCH, CONV_WIDTH - 1, CONV_CH), F32),
        ],
        scratch_shapes=[
            pltpu.VMEM((CONV_HALO + CONV_ROWS, CONV_CH), F32),
            pltpu.VMEM((SUBLANES, CONV_HALO + CONV_ROWS, LANES), F32),
            pltpu.VMEM((CONV_ROWS, CONV_CH), F32),
        ],
        compiler_params=_params("parallel", "arbitrary"),
        name="conv_prompt",
    )(u, u, w, b, g, bl)


def _cross_prompt_body(q_ref, k_ref, v_ref, o_ref):
    for h in range(TN // X_HEAD_DIM):
        hs = slice(h * X_HEAD_DIM, (h + 1) * X_HEAD_DIM)
        s = _dot_t(q_ref[:, hs].astype(BF16), k_ref[:, hs].astype(BF16)) * (X_HEAD_DIM ** -0.5)
        _, p, den = _softmax_parts(s)
        o_ref[:, hs] = _dot(p.astype(BF16), v_ref[:, hs].astype(BF16)) / den


def _cross_prompt(qx, mkv, *, tq):
    n_t = SEQ // tq
    n_hp = X_W // TN
    mkv3 = mkv.reshape(BATCH, N_MEM, 2 * X_W)
    return pl.pallas_call(
        _cross_prompt_body,
        grid=(BATCH, n_t, n_hp),
        in_specs=[
            pl.BlockSpec((tq, TN), lambda b, t, hp: (b * n_t + t, hp)),
            pl.BlockSpec((None, N_MEM, TN), lambda b, t, hp: (b, 0, hp)),
            pl.BlockSpec((None, N_MEM, TN), lambda b, t, hp: (b, 0, n_hp + hp)),
        ],
        out_specs=pl.BlockSpec((tq, TN), lambda b, t, hp: (b * n_t + t, hp)),
        out_shape=jax.ShapeDtypeStruct((BATCH * SEQ, X_W), F32),
        compiler_params=_params("parallel", "parallel", "arbitrary"),
        name="cross_prompt",
    )(qx, mkv3, mkv3)


def _post_body(c_ref, o0_ref, o1_ref, o2_ref, l0_ref, l1_ref, l2_ref, om_ref, ga_ref, gb_ref, gc_ref,
               wc_ref, wa_ref, wx_ref, wo_ref, x_ref, out_ref, oatt_ref, omem_ref, merged_ref, *, n_n):
    n = pl.program_id(1)

    @pl.when(n == 0)
    def _():
        for h in range(HEADS_PER_GROUP):
            hs = slice(h * HEAD_DIM, (h + 1) * HEAD_DIM)
            l0, l1, l2 = l0_ref[:, h:h + 1], l1_ref[:, h:h + 1], l2_ref[:, h:h + 1]
            m = jnp.maximum(jnp.maximum(l0, l1), l2)
            e0, e1, e2 = jnp.exp(l0 - m), jnp.exp(l1 - m), jnp.exp(l2 - m)
            mix = (e0 * o0_ref[:, hs] + e1 * o1_ref[:, hs] + e2 * o2_ref[:, hs]) / (e0 + e1 + e2)
            oatt_ref[:, hs] = mix.astype(BF16)
        omem_ref[...] = om_ref[...].astype(BF16)

    merged = (ga_ref[...] * _dot(c_ref[...], wc_ref[...])
              + gb_ref[...] * _dot(oatt_ref[...], wa_ref[...])
              + gc_ref[...] * _dot(omem_ref[...], wx_ref[...]))
    merged_ref[:, pl.ds(pl.multiple_of(n * TN, TN), TN)] = merged.astype(BF16)

    @pl.when(n == n_n - 1)
    def _():
        out_ref[...] = x_ref[...] + _dot(merged_ref[...], wo_ref[...])


def _post(c_act, o_g, lse_g, o_mem, sg, wc, wa, wx, wo, x, *, tm):
    m = x.shape[0]
    n_n = D_MODEL // TN
    row = lambda w: pl.BlockSpec((tm, w), lambda i, n: (i, 0))
    gate = lambda br: pl.BlockSpec((tm, TN), lambda i, n: (i, br * n_n + n))
    wcol = lambda k: pl.BlockSpec((k, TN), lambda i, n: (0, n))
    return pl.pallas_call(
        functools.partial(_post_body, n_n=n_n),
        grid=(m // tm, n_n),
        in_specs=[
            row(CONV_CH), row(ATT_OUT), row(ATT_OUT), row(ATT_OUT), row(LANES), row(LANES), row(LANES), row(X_W),
            gate(0), gate(1), gate(2),
            wcol(CONV_CH), wcol(ATT_OUT), wcol(X_W),
            pl.BlockSpec((D_MODEL, D_MODEL), lambda i, n: (0, 0), pipeline_mode=pl.Buffered(1)),
            row(D_MODEL),
        ],
        out_specs=row(D_MODEL),
        out_shape=jax.ShapeDtypeStruct((m, D_MODEL), F32),
        scratch_shapes=[
            pltpu.VMEM((tm, ATT_OUT), BF16),
            pltpu.VMEM((tm, X_W), BF16),
            pltpu.VMEM((tm, D_MODEL), BF16),
        ],
        compiler_params=_params("parallel", "arbitrary"),
        name="post",
    )(c_act, *o_g, *lse_g, o_mem, sg, sg, sg, wc, wa, wx, wo, x)


def _mix_sample_body(qkv_ref, qx_ref, u_ref, w0_ref, w1_ref, w2_ref, cs_ref, mem_ref, wdw_ref, b_ref, g_ref, bl_ref,
                     o0_ref, o1_ref, o2_ref, l0_ref, l1_ref, l2_ref, c_ref, om_ref, ncs_ref):
    scale = HEAD_DIM ** -0.5
    for gi, (win_ref, o_ref, l_ref) in enumerate(((w0_ref, o0_ref, l0_ref), (w1_ref, o1_ref, l1_ref),
                                                  (w2_ref, o2_ref, l2_ref))):
        lses = []
        for h in range(HEADS_PER_GROUP):
            col = gi * ATT_OUT + h * HEAD_DIM
            q = qkv_ref[:, col:col + HEAD_DIM]
            k_new = qkv_ref[:, ATT_W + col:ATT_W + col + HEAD_DIM]
            v_new = qkv_ref[:, 2 * ATT_W + col:2 * ATT_W + col + HEAD_DIM]
            k_buf, v_buf = win_ref[:, 0, h, :], win_ref[:, 1, h, :]
            s_buf = jnp.sum(k_buf * q, axis=-1, keepdims=True) * scale
            s_new = jnp.sum(k_new * q, axis=-1, keepdims=True) * scale
            m = jnp.maximum(jnp.max(s_buf, axis=0, keepdims=True), s_new)
            p_buf, p_new = jnp.exp(s_buf - m), jnp.exp(s_new - m)
            den = jnp.sum(p_buf, axis=0, keepdims=True) + p_new
            o = (jnp.sum(p_buf * v_buf, axis=0, keepdims=True) + p_new * v_new) / den
            o_ref[:, h * HEAD_DIM:(h + 1) * HEAD_DIM] = o
            lses.append(m + jnp.log(den))
        l_ref[...] = _lane_pack(lses, 1)
    for h in range(X_HEADS):
        hs = slice(h * X_HEAD_DIM, (h + 1) * X_HEAD_DIM)
        s = jnp.sum(mem_ref[:, 0, h, :] * qx_ref[:, hs], axis=-1, keepdims=True) * (X_HEAD_DIM ** -0.5)
        p = jnp.exp(s - jnp.max(s, axis=0, keepdims=True))
        om_ref[:, hs] = jnp.sum(p * mem_ref[:, 1, h, :], axis=0, keepdims=True) / jnp.sum(p, axis=0, keepdims=True)
    hist = CONV_WIDTH - 1
    c = (jnp.sum(cs_ref[...] * wdw_ref[0:hist, :], axis=0, keepdims=True)
         + u_ref[...] * wdw_ref[hist:hist + 1, :] + b_ref[...])
    c_ref[...] = _ln_swish(c, g_ref, bl_ref)
    ncs_ref[0:hist - 1, :] = cs_ref[1:hist, :]
    ncs_ref[hist - 1:hist, :] = u_ref[...]


def _mix_sample(layer, qkv, qx, u, wins, conv_state, mem_kv, w, b, g, bl):
    row = lambda wd: pl.BlockSpec((None, 1, wd), lambda s: (s, 0, 0))
    vec = pl.BlockSpec((1, CONV_CH), lambda s: (0, 0))
    hist = CONV_WIDTH - 1
    win_specs, win_args = [], []
    for (win, dil), arr in zip(GROUPS, wins):
        win_args.append(arr.reshape(DEPTH, DEC_BATCH, win // dil, dil, 2, HEADS_PER_GROUP, HEAD_DIM))
        win_specs.append(pl.BlockSpec((None, None, win // dil, None, 2, HEADS_PER_GROUP, HEAD_DIM),
                                      lambda s: (layer, s, 0, 0, 0, 0, 0)))
    f32_row = lambda wd: jax.ShapeDtypeStruct((DEC_BATCH, 1, wd), F32)
    outs = pl.pallas_call(
        _mix_sample_body,
        grid=(DEC_BATCH,),
        in_specs=[row(QKV_W), row(X_W), row(CONV_CH), *win_specs,
                  pl.BlockSpec((None, None, hist, CONV_CH), lambda s: (layer, s, 0, 0)),
                  pl.BlockSpec((None, None, N_MEM, 2, X_HEADS, X_HEAD_DIM), lambda s: (layer, s, 0, 0, 0, 0)),
                  pl.BlockSpec((CONV_WIDTH, CONV_CH), lambda s: (0, 0)), vec, vec, vec],
        out_specs=[row(ATT_OUT)] * 3 + [row(LANES)] * 3 + [row(CONV_CH), row(X_W),
                                                            pl.BlockSpec((None, hist, CONV_CH), lambda s: (s, 0, 0))],
        out_shape=[f32_row(ATT_OUT)] * 3 + [f32_row(LANES)] * 3 + [
            f32_row(CONV_CH), f32_row(X_W), jax.ShapeDtypeStruct((DEC_BATCH, hist, CONV_CH), F32)],
        compiler_params=_params("parallel"),
        name="mix_sample",
    )(qkv.reshape(DEC_BATCH, 1, QKV_W), qx.reshape(DEC_BATCH, 1, X_W), u.reshape(DEC_BATCH, 1, CONV_CH),
      *win_args, conv_state, mem_kv, w, b, g, bl)
    flat = [o.reshape(DEC_BATCH, -1) for o in outs[:8]]
    return flat[0:3], flat[3:6], flat[6].astype(BF16), flat[7], outs[8]


def _shift_window_body(x_ref, nxt_ref, new_ref, o_ref, *, n_c):
    rows = x_ref.shape[0]
    o_ref[0:rows - 1] = x_ref[1:rows]
    last = pl.program_id(2) == n_c - 1
    o_ref[rows - 1:rows] = jnp.where(last, new_ref[...], nxt_ref[...])


def _shift_window(state, new):
    rows = state.shape[2]
    rc = min(rows, SHIFT_ROWS)
    n_c = rows // rc
    tail = state.shape[3:]
    zeros = (0,) * len(tail)
    one = pl.BlockSpec((None, None, 1) + tail, lambda l, b, c: (l, b, jnp.minimum((c + 1) * rc, rows - 1)) + zeros)
    return pl.pallas_call(
        functools.partial(_shift_window_body, n_c=n_c),
        grid=(DEPTH, DEC_BATCH, n_c),
        in_specs=[pl.BlockSpec((None, None, rc) + tail, lambda l, b, c: (l, b, c) + zeros),
                  one,
                  pl.BlockSpec((None, None, 1) + tail, lambda l, b, c: (l, b, 0) + zeros)],
        out_specs=pl.BlockSpec((None, None, rc) + tail, lambda l, b, c: (l, b, c) + zeros),
        out_shape=jax.ShapeDtypeStruct(state.shape, state.dtype),
        compiler_params=_params("parallel", "parallel", "arbitrary"),
        name="shift_window",
    )(state, state, new)


def kernel(x_prompt, x_sample, state_win0, state_win1, state_win2, state_conv, cache_mem_kv, mem_prompt, w_ffn1_norm, w_ffn1_gate, w_ffn1_up, w_ffn1_down, w_mix_norm, w_in, w_dw, b_dw, g_cln, b_cln, w_conv_out, w_att_out, w_x_out, w_o, w_mem_norm, w_mem_kv, w_ffn2_norm, w_ffn2_gate, w_ffn2_up, w_ffn2_down, w_final_norm):
    state_wins = (state_win0, state_win1, state_win2)
    hp = x_prompt.reshape(BATCH * SEQ, D_MODEL)
    hs = x_sample.reshape(DEC_BATCH, D_MODEL)
    mem = mem_prompt.reshape(BATCH * N_MEM, D_MODEL)
    tab_p = _rope_tables(jnp.arange(SEQ))
    tab_s = _rope_tables(jnp.full((DEC_BATCH,), PAST_LEN))
    vec = lambda a: a.reshape(1, -1)
    g_final = vec(w_final_norm)
    ffn1_w = tuple(a.astype(BF16) for a in (w_ffn1_gate, w_ffn1_up, w_ffn1_down))
    ffn2_w = tuple(a.astype(BF16) for a in (w_ffn2_gate, w_ffn2_up, w_ffn2_down))
    w_in_bf, w_mem_kv_bf = w_in.astype(BF16), w_mem_kv.astype(BF16)

    win_p = [[] for _ in GROUPS]
    kv_s = [[] for _ in GROUPS]
    conv_p, conv_s, mem_p = [], [], []
    for l in range(DEPTH):
        last = l == DEPTH - 1
        conv_w = (w_dw[l], vec(b_dw[l]), vec(g_cln[l]), vec(b_cln[l]))
        out_w = tuple(a[l].astype(BF16) for a in (w_conv_out, w_att_out, w_x_out, w_o))

        mkv = _mem_proj(l, mem, vec(w_mem_norm[l]), w_mem_kv_bf)
        mem_p.append(mkv.reshape(BATCH, N_MEM, 2, X_HEADS, X_HEAD_DIM))

        hp, hs = _ffn(l, hp, hs, vec(w_ffn1_norm[l]), *ffn1_w)
        (qkv, u, qx, sg), (qkv_s, u_s, qx_s, sg_s) = _in_proj(l, hp, hs, vec(w_mix_norm[l]), w_in_bf, tab_p, tab_s)

        att = [_attn_prompt(qkv, gi) for gi in range(N_GROUPS)]
        c_act, conv_tail = _conv_prompt(u, *conv_w)
        o_mem = _cross_prompt(qx, mkv, tq=1024)
        hp = _post(c_act, [a[0] for a in att], [a[1] for a in att], o_mem, sg, *out_w, hp, tm=256)
        qkv3 = qkv.reshape(BATCH, SEQ, QKV_W)
        for gi, (win, _) in enumerate(GROUPS):
            keep = min(win, SEQ)
            cut = lambda c0: qkv3[:, SEQ - keep:, c0 + gi * ATT_OUT:c0 + (gi + 1) * ATT_OUT].reshape(
                BATCH, keep, HEADS_PER_GROUP, HEAD_DIM)
            win_p[gi].append(jnp.stack([cut(ATT_W), cut(2 * ATT_W)], axis=2))
        conv_p.append(conv_tail)

        o_g, lse_g, c_act, o_mem, conv_new = _mix_sample(l, qkv_s, qx_s, u_s, state_wins, state_conv, cache_mem_kv,
                                                         *conv_w)
        hs = _post(c_act, o_g, lse_g, o_mem, sg_s, *out_w, hs, tm=DEC_BATCH)
        for gi in range(N_GROUPS):
            cut = lambda c0: qkv_s[:, c0 + gi * ATT_OUT:c0 + (gi + 1) * ATT_OUT].reshape(
                DEC_BATCH, HEADS_PER_GROUP, HEAD_DIM)
            kv_s[gi].append(jnp.stack([cut(ATT_W), cut(2 * ATT_W)], axis=1))
        conv_s.append(conv_new)

        hp, hs = _ffn(l, hp, hs, vec(w_ffn2_norm[l]), *ffn2_w, g_final if last else None)

    st = lambda xs: jnp.stack(xs, axis=0)
    win_s = [_shift_window(s, st(k)[:, :, None]) for s, k in zip(state_wins, kv_s)]
    return (hp.reshape(BATCH, SEQ, D_MODEL), hs.reshape(DEC_BATCH, 1, D_MODEL),
            st(win_p[0]), st(win_p[1]), st(win_p[2]), st(conv_p), st(mem_p),
            win_s[0], win_s[1], win_s[2], st(conv_s))
```

```python
import functools

import jax
import jax.numpy as jnp
from jax import lax
from jax.experimental import pallas as pl
from jax.experimental.pallas import tpu as pltpu

D_MODEL = 2048
BATCH = 2
SEQ = 4096
DEPTH = 2
DEC_BATCH = 32
PAST_LEN = 8192
HEAD_DIM = 128
HEADS_PER_GROUP = 4
GROUPS = ((128, 1), (512, 4), (2048, 16))
N_GROUPS = len(GROUPS)
ATT_W = N_GROUPS * HEADS_PER_GROUP * HEAD_DIM
ATT_OUT = HEADS_PER_GROUP * HEAD_DIM
BLOCK = 128
ROT_DIM = HEAD_DIM // 4
ROPE_THETA = 500000.0
CONV_CH = 3 * D_MODEL // 4
CONV_WIDTH = 31
X_HEADS = 4
X_HEAD_DIM = D_MODEL // 8
X_W = X_HEADS * X_HEAD_DIM
N_MEM = 256
D_FF = 256 * ((8 * D_MODEL // 3 + 255) // 256)
N_BRANCH = 3
QKV_W = 3 * ATT_W
COL_A = QKV_W
COL_QX = COL_A + 2 * CONV_CH
COL_GATES = COL_QX + X_W
IN_W = COL_GATES + N_BRANCH * D_MODEL
EPS = 1e-6
NEG = -1e30

F32 = jnp.float32
BF16 = jnp.bfloat16

LANES = 128
SUBLANES = 8
VMEM_LIMIT_BYTES = 58 * 1024 * 1024
TN = 512
TF = 512
TF_FINAL = 256
TM = 1024
EDGE_ROWS = 256
CONV_ROWS = 128
CONV_HALO = 32
CONV_PASS = 64
LN_ROWS = 16
ATT_TQ = (256, 512, 512)
SHIFT_ROWS = 512


def _params(*sem):
    return pltpu.CompilerParams(dimension_semantics=sem, vmem_limit_bytes=VMEM_LIMIT_BYTES)


def _rms(x, g):
    return x * lax.rsqrt(jnp.mean(x * x, axis=-1, keepdims=True) + EPS) * g


def _row_chunks(n):
    step = min(n, EDGE_ROWS)
    return [slice(r, r + step) for r in range(0, n, step)]


def _norm_rows(x_ref, g_ref, h_ref):
    for r in _row_chunks(x_ref.shape[0]):
        h_ref[r, :] = _rms(x_ref[r, :], g_ref[...]).astype(BF16)


def _dot(a, b):
    return jnp.dot(a, b, preferred_element_type=F32)


def _dot_t(a, b):
    return lax.dot_general(a, b, (((1,), (1,)), ((), ())), preferred_element_type=F32)


def _ffn_body(xp_ref, xs_ref, g_ref, wg_ref, wu_ref, wd_ref, *rest, n_f, final_norm):
    if final_norm:
        gf_ref, op_ref, os_ref, hp_ref, hs_ref = rest
    else:
        op_ref, os_ref, hp_ref, hs_ref = rest
    i, f = pl.program_id(0), pl.program_id(1)
    wg, wu, wd = wg_ref[...], wu_ref[...], wd_ref

    def rows(x_ref, o_ref, h_ref):
        @pl.when(f == 0)
        def _():
            _norm_rows(x_ref, g_ref, h_ref)
            o_ref[...] = jnp.zeros_like(o_ref)

        h = h_ref[...]
        gate = _dot(h, wg)
        act = (gate * jax.nn.sigmoid(gate) * _dot(h, wu)).astype(BF16)
        for n in range(0, D_MODEL, TN):
            o_ref[:, n:n + TN] += _dot(act, wd[:, n:n + TN])

        @pl.when(f == n_f - 1)
        def _():
            for r in _row_chunks(x_ref.shape[0]):
                y = x_ref[r, :] + 0.5 * o_ref[r, :]
                if final_norm:
                    y = _rms(y, gf_ref[...])
                o_ref[r, :] = y

    rows(xp_ref, op_ref, hp_ref)

    @pl.when(i == 0)
    def _():
        rows(xs_ref, os_ref, hs_ref)


def _ffn(layer, xp, xs, g, wg, wu, wd, gf=None):
    m, ms = xp.shape[0], xs.shape[0]
    final_norm = gf is not None
    tf = TF_FINAL if final_norm else TF
    n_f = D_FF // tf
    vec = pl.BlockSpec((1, D_MODEL), lambda i, f: (0, 0))
    in_specs = [
        pl.BlockSpec((TM, D_MODEL), lambda i, f: (i, 0), pipeline_mode=pl.Buffered(1)),
        pl.BlockSpec((ms, D_MODEL), lambda i, f: (0, 0)),
        vec,
        pl.BlockSpec((None, D_MODEL, tf), lambda i, f: (layer, 0, f)),
        pl.BlockSpec((None, D_MODEL, tf), lambda i, f: (layer, 0, f)),
        pl.BlockSpec((None, tf, D_MODEL), lambda i, f: (layer, f, 0)),
    ]
    args = [xp, xs, g, wg, wu, wd]
    if final_norm:
        in_specs.append(vec)
        args.append(gf)
    return pl.pallas_call(
        functools.partial(_ffn_body, n_f=n_f, final_norm=final_norm),
        grid=(m // TM, n_f),
        in_specs=in_specs,
        out_specs=[pl.BlockSpec((TM, D_MODEL), lambda i, f: (i, 0)),
                   pl.BlockSpec((ms, D_MODEL), lambda i, f: (0, 0))],
        out_shape=[jax.ShapeDtypeStruct((m, D_MODEL), F32), jax.ShapeDtypeStruct((ms, D_MODEL), F32)],
        scratch_shapes=[pltpu.VMEM((TM, D_MODEL), BF16), pltpu.VMEM((ms, D_MODEL), BF16)],
        compiler_params=_params("arbitrary", "arbitrary"),
        name="ffn",
    )(*args)


def _rope(z, c_ref, s1_ref, s2_ref):
    half = ROT_DIM // 2
    c, s1, s2 = c_ref[...], s1_ref[...], s2_ref[...]
    heads = []
    for h in range(z.shape[-1] // HEAD_DIM):
        zh = z[:, h * HEAD_DIM:(h + 1) * HEAD_DIM]
        heads.append(zh * c + pltpu.roll(zh, HEAD_DIM - half, 1) * s1 + pltpu.roll(zh, half, 1) * s2)
    return jnp.concatenate(heads, axis=1)


def _col_tile(ref, t):
    return ref.at[:, pl.ds(pl.multiple_of(t * TN, TN), TN)]


def _proj_qkvu_body(xp_ref, xs_ref, g_ref, w_ref, w2_ref, cp_ref, s1p_ref, s2p_ref, cs_ref, s1s_ref, s2s_ref,
                    qkvp_ref, up_ref, qkvs_ref, us_ref, hp_ref, hs_ref):
    i, s = pl.program_id(0), pl.program_id(1)
    n_rot, n_qkv = 2 * ATT_W // TN, QKV_W // TN
    w = w_ref[...]

    def rows(x_ref, h_ref, tabs, qkv_ref, u_ref):
        @pl.when(s == 0)
        def _():
            _norm_rows(x_ref, g_ref, h_ref)

        h = h_ref[...]
        z = _dot(h, w)

        @pl.when(s < n_rot)
        def _():
            qkv_ref[...] = _rope(z, *tabs)

        @pl.when((s >= n_rot) & (s < n_qkv))
        def _():
            qkv_ref[...] = z

        @pl.when(s >= n_qkv)
        def _():
            u_ref[...] = z * jax.nn.sigmoid(_dot(h, w2_ref[...]))

    rows(xp_ref, hp_ref, (cp_ref, s1p_ref, s2p_ref), qkvp_ref, up_ref)

    @pl.when(i == 0)
    def _():
        rows(xs_ref, hs_ref, (cs_ref, s1s_ref, s2s_ref),
             _col_tile(qkvs_ref, jnp.minimum(s, n_qkv - 1)), _col_tile(us_ref, jnp.maximum(s - n_qkv, 0)))


def _proj_qxg_body(xp_ref, xs_ref, g_ref, w_ref, qxp_ref, sgp_ref, qxs_ref, sgs_ref, hp_ref, hs_ref):
    i, s = pl.program_id(0), pl.program_id(1)
    n_qx = X_W // TN
    w = w_ref[...]

    def rows(x_ref, h_ref, qx_ref, sg_ref):
        @pl.when(s == 0)
        def _():
            _norm_rows(x_ref, g_ref, h_ref)

        z = _dot(h_ref[...], w)

        @pl.when(s < n_qx)
        def _():
            qx_ref[...] = z

        @pl.when(s >= n_qx)
        def _():
            sg_ref[...] = jax.nn.sigmoid(z).astype(BF16)

    rows(xp_ref, hp_ref, qxp_ref, sgp_ref)

    @pl.when(i == 0)
    def _():
        rows(xs_ref, hs_ref, _col_tile(qxs_ref, jnp.minimum(s, n_qx - 1)), _col_tile(sgs_ref, jnp.maximum(s - n_qx, 0)))


def _in_proj(layer, xp, xs, g, w_in, tab_p, tab_s):
    m, ms = xp.shape[0], xs.shape[0]
    n_tab = tab_p[0].shape[0] // TM
    n_qkv, n_u, n_qx, n_g = QKV_W // TN, CONV_CH // TN, X_W // TN, N_BRANCH * D_MODEL // TN
    a0, qx0 = COL_A // TN, COL_QX // TN
    xspec = [pl.BlockSpec((TM, D_MODEL), lambda i, s: (i, 0)),
             pl.BlockSpec((ms, D_MODEL), lambda i, s: (0, 0)),
             pl.BlockSpec((1, D_MODEL), lambda i, s: (0, 0))]
    wspec = lambda fn: pl.BlockSpec((None, D_MODEL, TN), lambda i, s: (layer, 0, fn(s)))
    ptile = lambda fn: pl.BlockSpec((TM, TN), lambda i, s: (i, fn(s)))
    whole = lambda wd: pl.BlockSpec((ms, wd), lambda i, s: (0, 0))
    shp = lambda r, wd: jax.ShapeDtypeStruct((r, wd), F32)
    scratch = [pltpu.VMEM((TM, D_MODEL), BF16), pltpu.VMEM((ms, D_MODEL), BF16)]

    qkv_p, u_p, qkv_s, u_s = pl.pallas_call(
        _proj_qkvu_body,
        grid=(m // TM, n_qkv + n_u),
        in_specs=xspec + [
            wspec(lambda s: s),
            wspec(lambda s: jnp.clip(s + n_u, a0 + n_u, a0 + 2 * n_u - 1)),
            *[pl.BlockSpec((TM, HEAD_DIM), lambda i, s: (i % n_tab, 0)) for _ in range(3)],
            *[pl.BlockSpec((ms, HEAD_DIM), lambda i, s: (0, 0)) for _ in range(3)],
        ],
        out_specs=[ptile(lambda s: jnp.minimum(s, n_qkv - 1)), ptile(lambda s: jnp.maximum(s - n_qkv, 0)),
                   whole(QKV_W), whole(CONV_CH)],
        out_shape=[shp(m, QKV_W), shp(m, CONV_CH), shp(ms, QKV_W), shp(ms, CONV_CH)],
        scratch_shapes=scratch,
        compiler_params=_params("arbitrary", "arbitrary"),
        name="proj_qkvu",
    )(xp, xs, g, w_in, w_in, *tab_p, *tab_s)

    qx_p, sg_p, qx_s, sg_s = pl.pallas_call(
        _proj_qxg_body,
        grid=(m // TM, n_qx + n_g),
        in_specs=xspec + [wspec(lambda s: qx0 + s)],
        out_specs=[ptile(lambda s: jnp.minimum(s, n_qx - 1)), ptile(lambda s: jnp.maximum(s - n_qx, 0)),
                   whole(X_W), whole(N_BRANCH * D_MODEL)],
        out_shape=[shp(m, X_W), jax.ShapeDtypeStruct((m, N_BRANCH * D_MODEL), BF16),
                   shp(ms, X_W), jax.ShapeDtypeStruct((ms, N_BRANCH * D_MODEL), BF16)],
        scratch_shapes=scratch,
        compiler_params=_params("arbitrary", "arbitrary"),
        name="proj_qxg",
    )(xp, xs, g, w_in)
    return (qkv_p, u_p, qx_p, sg_p), (qkv_s, u_s, qx_s, sg_s)


def _mem_proj_body(x_ref, g_ref, w_ref, o_ref, h_ref):
    @pl.when(pl.program_id(0) == 0)
    def _():
        h_ref[...] = _rms(x_ref[...], g_ref[...]).astype(BF16)

    o_ref[...] = _dot(h_ref[...], w_ref[...])


def _mem_proj(layer, x, g, w):
    m, n = x.shape[0], w.shape[-1]
    return pl.pallas_call(
        _mem_proj_body,
        grid=(n // TN,),
        in_specs=[pl.BlockSpec((m, D_MODEL), lambda j: (0, 0)),
                  pl.BlockSpec((1, D_MODEL), lambda j: (0, 0)),
                  pl.BlockSpec((None, D_MODEL, TN), lambda j: (layer, 0, j))],
        out_specs=pl.BlockSpec((m, TN), lambda j: (0, j)),
        out_shape=jax.ShapeDtypeStruct((m, n), F32),
        scratch_shapes=[pltpu.VMEM((m, D_MODEL), BF16)],
        compiler_params=_params("arbitrary"),
        name="mem_proj",
    )(x, g, w)


def _rope_tables(pos):
    half = ROT_DIM // 2
    inv = ROPE_THETA ** (-jnp.arange(half, dtype=F32) / half)
    ang = pos.astype(F32)[:, None] * inv[None, :]
    cos, sin = jnp.cos(ang), jnp.sin(ang)
    n = pos.shape[0]
    zeros = jnp.zeros((n, half), F32)
    rest0 = jnp.zeros((n, HEAD_DIM - ROT_DIM), F32)
    c = jnp.concatenate([cos, cos, jnp.ones((n, HEAD_DIM - ROT_DIM), F32)], axis=1)
    s1 = jnp.concatenate([-sin, zeros, rest0], axis=1)
    s2 = jnp.concatenate([zeros, sin, rest0], axis=1)
    return c, s1, s2


def _softmax_parts(s):
    m = jnp.max(s, axis=-1, keepdims=True)
    p = jnp.exp(s - m)
    return m, p, jnp.sum(p, axis=-1, keepdims=True)


def _lane_pack(cols, rows):
    lane = lax.broadcasted_iota(jnp.int32, (rows, LANES), 1)
    out = jnp.zeros((rows, LANES), F32)
    for h, c in enumerate(cols):
        out = jnp.where(lane == h, c, out)
    return out


def _attn_prompt_body(q_ref, k_ref, v_ref, o_ref, lse_ref, *, win, dil, tq):
    t0 = pl.program_id(1) * tq
    span = win + tq
    start = pl.multiple_of(jnp.maximum(t0 - win, 0), min(win, tq))
    qpos = t0 + lax.broadcasted_iota(jnp.int32, (tq, span), 0)
    kpos = start + lax.broadcasted_iota(jnp.int32, (tq, span), 1)
    dist = qpos - kpos
    mask = (dist >= 0) & (dist <= win) & ((dist & (dil - 1)) == 0)
    lses = []
    for h in range(HEADS_PER_GROUP):
        hs = slice(h * HEAD_DIM, (h + 1) * HEAD_DIM)
        q = (q_ref[:, hs] * (HEAD_DIM ** -0.5)).astype(BF16)
        k = k_ref[pl.ds(start, span), hs].astype(BF16)
        v = v_ref[pl.ds(start, span), hs].astype(BF16)
        s = jnp.where(mask, _dot_t(q, k), NEG)
        m, p, den = _softmax_parts(s)
        o_ref[:, hs] = _dot(p.astype(BF16), v) / den
        lses.append(m + jnp.log(den))
    lse_ref[...] = _lane_pack(lses, tq)


def _attn_prompt(qkv, gi):
    win, dil = GROUPS[gi]
    assert dil & (dil - 1) == 0 and win + ATT_TQ[gi] <= SEQ
    tq = ATT_TQ[gi]
    n_t = SEQ // tq
    kt, vt = ATT_W // ATT_OUT + gi, 2 * ATT_W // ATT_OUT + gi
    return pl.pallas_call(
        functools.partial(_attn_prompt_body, win=win, dil=dil, tq=tq),
        grid=(BATCH, n_t),
        in_specs=[
            pl.BlockSpec((tq, ATT_OUT), lambda b, t: (b * n_t + t, gi)),
            pl.BlockSpec((SEQ, ATT_OUT), lambda b, t: (b, kt), pipeline_mode=pl.Buffered(1)),
            pl.BlockSpec((SEQ, ATT_OUT), lambda b, t: (b, vt), pipeline_mode=pl.Buffered(1)),
        ],
        out_specs=[
            pl.BlockSpec((tq, ATT_OUT), lambda b, t: (b * n_t + t, 0)),
            pl.BlockSpec((tq, LANES), lambda b, t: (b * n_t + t, 0)),
        ],
        out_shape=[
            jax.ShapeDtypeStruct((BATCH * SEQ, ATT_OUT), F32),
            jax.ShapeDtypeStruct((BATCH * SEQ, LANES), F32),
        ],
        compiler_params=_params("parallel", "arbitrary"),
        name="attn_prompt_g%d" % gi,
    )(qkv, qkv, qkv)


def _ln_swish(c, g_ref, b_ref):
    mu = jnp.mean(c, axis=-1, keepdims=True)
    xc = c - mu
    y = xc * lax.rsqrt(jnp.mean(xc * xc, axis=-1, keepdims=True) + EPS) * g_ref[...] + b_ref[...]
    return y * jax.nn.sigmoid(y)


def _conv_prompt_body(u_ref, halo_ref, w_ref, b_ref, g_ref, bl_ref, o_ref, tail_ref, ext_ref, sh_ref, c_ref, *, n_t):
    t = pl.program_id(1)
    halo = halo_ref[...]
    ext_ref[0:CONV_HALO, :] = jnp.where(t > 0, halo, jnp.zeros_like(halo))
    ext_ref[CONV_HALO:, :] = u_ref[...]
    first = CONV_HALO - (CONV_WIDTH - 1)
    n_sh = CONV_HALO + CONV_ROWS - SUBLANES
    for c in range(CONV_CH // LANES):
        cs = slice(c * LANES, (c + 1) * LANES)
        for s in range(1, SUBLANES):
            sh_ref[s, 0:n_sh, :] = ext_ref[s:s + n_sh, cs]
        for r0 in range(0, CONV_ROWS, CONV_PASS):
            acc = jnp.zeros((CONV_PASS, LANES), F32)
            for k in range(CONV_WIDTH):
                s, q = (first + k) % SUBLANES, r0 + (first + k) // SUBLANES * SUBLANES
                rows = ext_ref[q:q + CONV_PASS, cs] if s == 0 else sh_ref[s, q:q + CONV_PASS, :]
                acc = acc + rows * w_ref[k:k + 1, cs]
            c_ref[r0:r0 + CONV_PASS, cs] = acc
    for r0 in range(0, CONV_ROWS, LN_ROWS):
        rs = slice(r0, r0 + LN_ROWS)
        o_ref[rs, :] = _ln_swish(c_ref[rs, :] + b_ref[...], g_ref, bl_ref).astype(BF16)

    @pl.when(t == n_t - 1)
    def _():
        tail_ref[...] = u_ref[CONV_ROWS - (CONV_WIDTH - 1):, :]


def _conv_prompt(u, w, b, g, bl):
    n_t = SEQ // CONV_ROWS
    hpb = CONV_ROWS // CONV_HALO
    vec = pl.BlockSpec((1, CONV_CH), lambda b_, t: (0, 0))
    return pl.pallas_call(
        functools.partial(_conv_prompt_body, n_t=n_t),
        grid=(BATCH, n_t),
        in_specs=[
            pl.BlockSpec((CONV_ROWS, CONV_CH), lambda b_, t: (b_ * n_t + t, 0)),
            pl.BlockSpec((CONV_HALO, CONV_CH), lambda b_, t: (jnp.maximum((b_ * n_t + t) * hpb - 1, 0), 0)),
            pl.BlockSpec((CONV_WIDTH, CONV_CH), lambda b_, t: (0, 0)),
            vec, vec, vec,
        ],
        out_specs=[
            pl.BlockSpec((CONV_ROWS, CONV_CH), lambda b_, t: (b_ * n_t + t, 0)),
            pl.BlockSpec((None, CONV_WIDTH - 1, CONV_CH), lambda b_, t: (b_, 0, 0)),
        ],
        out_shape=[
            jax.ShapeDtypeStruct((BATCH * SEQ, CONV_CH), BF16),
            jax.ShapeDtypeStruct((BATCH, CONV_WIDTH - 1, CONV_CH), F32),
        ],
        scratch_shapes=[
            pltpu.VMEM((CONV_HALO + CONV_ROWS, CONV_CH), F32),
            pltpu.VMEM((SUBLANES, CONV_HALO + CONV_ROWS, LANES), F32),
            pltpu.VMEM((CONV_ROWS, CONV_CH), F32),
        ],
        compiler_params=_params("parallel", "arbitrary"),
        name="conv_prompt",
    )(u, u, w, b, g, bl)


def _cross_prompt_body(q_ref, k_ref, v_ref, o_ref):
    for h in range(TN // X_HEAD_DIM):
        hs = slice(h * X_HEAD_DIM, (h + 1) * X_HEAD_DIM)
        s = _dot_t(q_ref[:, hs].astype(BF16), k_ref[:, hs].astype(BF16)) * (X_HEAD_DIM ** -0.5)
        _, p, den = _softmax_parts(s)
        o_ref[:, hs] = _dot(p.astype(BF16), v_ref[:, hs].astype(BF16)) / den


def _cross_prompt(qx, mkv, *, tq):
    n_t = SEQ // tq
    n_hp = X_W // TN
    mkv3 = mkv.reshape(BATCH, N_MEM, 2 * X_W)
    return pl.pallas_call(
        _cross_prompt_body,
        grid=(BATCH, n_t, n_hp),
        in_specs=[
            pl.BlockSpec((tq, TN), lambda b, t, hp: (b * n_t + t, hp)),
            pl.BlockSpec((None, N_MEM, TN), lambda b, t, hp: (b, 0, hp)),
            pl.BlockSpec((None, N_MEM, TN), lambda b, t, hp: (b, 0, n_hp + hp)),
        ],
        out_specs=pl.BlockSpec((tq, TN), lambda b, t, hp: (b * n_t + t, hp)),
        out_shape=jax.ShapeDtypeStruct((BATCH * SEQ, X_W), F32),
        compiler_params=_params("parallel", "parallel", "arbitrary"),
        name="cross_prompt",
    )(qx, mkv3, mkv3)


def _post_body(c_ref, o0_ref, o1_ref, o2_ref, l0_ref, l1_ref, l2_ref, om_ref, sg_ref, wc_ref, wa_ref, wx_ref, wo_ref,
               x_ref, out_ref, oatt_ref, merged_ref):
    for h in range(HEADS_PER_GROUP):
        hs = slice(h * HEAD_DIM, (h + 1) * HEAD_DIM)
        l0, l1, l2 = l0_ref[:, h:h + 1], l1_ref[:, h:h + 1], l2_ref[:, h:h + 1]
        m = jnp.maximum(jnp.maximum(l0, l1), l2)
        e0, e1, e2 = jnp.exp(l0 - m), jnp.exp(l1 - m), jnp.exp(l2 - m)
        mix = (e0 * o0_ref[:, hs] + e1 * o1_ref[:, hs] + e2 * o2_ref[:, hs]) / (e0 + e1 + e2)
        oatt_ref[:, hs] = mix.astype(BF16)
    omem = om_ref[...].astype(BF16)
    for n in range(0, D_MODEL, TN):
        cs = slice(n, n + TN)
        merged = (sg_ref[:, cs] * _dot(c_ref[...], wc_ref[:, cs])
                  + sg_ref[:, D_MODEL + n:D_MODEL + n + TN] * _dot(oatt_ref[...], wa_ref[:, cs])
                  + sg_ref[:, 2 * D_MODEL + n:2 * D_MODEL + n + TN] * _dot(omem, wx_ref[:, cs]))
        merged_ref[:, cs] = merged.astype(BF16)
    for n in range(0, D_MODEL, TN):
        cs = slice(n, n + TN)
        out_ref[:, cs] = x_ref[:, cs] + _dot(merged_ref[...], wo_ref[:, cs])


def _post(c_act, o_g, lse_g, o_mem, sg, wc, wa, wx, wo, x, *, tm):
    m = x.shape[0]
    row = lambda w: pl.BlockSpec((tm, w), lambda i: (i, 0))
    resident = lambda a: pl.BlockSpec(a.shape, lambda i: (0, 0), pipeline_mode=pl.Buffered(1))
    return pl.pallas_call(
        _post_body,
        grid=(m // tm,),
        in_specs=[
            row(CONV_CH), row(ATT_OUT), row(ATT_OUT), row(ATT_OUT), row(LANES), row(LANES), row(LANES), row(X_W),
            row(N_BRANCH * D_MODEL),
            resident(wc), resident(wa), resident(wx), resident(wo),
            row(D_MODEL),
        ],
        out_specs=row(D_MODEL),
        out_shape=jax.ShapeDtypeStruct((m, D_MODEL), F32),
        scratch_shapes=[
            pltpu.VMEM((tm, ATT_OUT), BF16),
            pltpu.VMEM((tm, D_MODEL), BF16),
        ],
        compiler_params=_params("arbitrary"),
        name="post",
    )(c_act, *o_g, *lse_g, o_mem, sg, wc, wa, wx, wo, x)


def _mix_sample_body(qkv_ref, qx_ref, u_ref, w0_ref, w1_ref, w2_ref, cs_ref, mem_ref, wdw_ref, b_ref, g_ref, bl_ref,
                     o0_ref, o1_ref, o2_ref, l0_ref, l1_ref, l2_ref, c_ref, om_ref, ncs_ref):
    scale = HEAD_DIM ** -0.5
    for gi, (win_ref, o_ref, l_ref) in enumerate(((w0_ref, o0_ref, l0_ref), (w1_ref, o1_ref, l1_ref),
                                                  (w2_ref, o2_ref, l2_ref))):
        lses = []
        for h in range(HEADS_PER_GROUP):
            col = gi * ATT_OUT + h * HEAD_DIM
            q = qkv_ref[:, col:col + HEAD_DIM]
            k_new = qkv_ref[:, ATT_W + col:ATT_W + col + HEAD_DIM]
            v_new = qkv_ref[:, 2 * ATT_W + col:2 * ATT_W + col + HEAD_DIM]
            k_buf, v_buf = win_ref[:, 0, h, :], win_ref[:, 1, h, :]
            s_buf = jnp.sum(k_buf * q, axis=-1, keepdims=True) * scale
            s_new = jnp.sum(k_new * q, axis=-1, keepdims=True) * scale
            m = jnp.maximum(jnp.max(s_buf, axis=0, keepdims=True), s_new)
            p_buf, p_new = jnp.exp(s_buf - m), jnp.exp(s_new - m)
            den = jnp.sum(p_buf, axis=0, keepdims=True) + p_new
            o = (jnp.sum(p_buf * v_buf, axis=0, keepdims=True) + p_new * v_new) / den
            o_ref[:, h * HEAD_DIM:(h + 1) * HEAD_DIM] = o
            lses.append(m + jnp.log(den))
        l_ref[...] = _lane_pack(lses, 1)
    for h in range(X_HEADS):
        hs = slice(h * X_HEAD_DIM, (h + 1) * X_HEAD_DIM)
        s = jnp.sum(mem_ref[:, 0, h, :] * qx_ref[:, hs], axis=-1, keepdims=True) * (X_HEAD_DIM ** -0.5)
        p = jnp.exp(s - jnp.max(s, axis=0, keepdims=True))
        om_ref[:, hs] = jnp.sum(p * mem_ref[:, 1, h, :], axis=0, keepdims=True) / jnp.sum(p, axis=0, keepdims=True)
    hist = CONV_WIDTH - 1
    c = (jnp.sum(cs_ref[...] * wdw_ref[0:hist, :], axis=0, keepdims=True)
         + u_ref[...] * wdw_ref[hist:hist + 1, :] + b_ref[...])
    c_ref[...] = _ln_swish(c, g_ref, bl_ref)
    ncs_ref[0:hist - 1, :] = cs_ref[1:hist, :]
    ncs_ref[hist - 1:hist, :] = u_ref[...]


def _mix_sample(layer, qkv, qx, u, wins, conv_state, mem_kv, w, b, g, bl):
    row = lambda wd: pl.BlockSpec((None, 1, wd), lambda s: (s, 0, 0))
    vec = pl.BlockSpec((1, CONV_CH), lambda s: (0, 0))
    hist = CONV_WIDTH - 1
    win_specs, win_args = [], []
    for (win, dil), arr in zip(GROUPS, wins):
        win_args.append(arr.reshape(DEPTH, DEC_BATCH, win // dil, dil, 2, HEADS_PER_GROUP, HEAD_DIM))
        win_specs.append(pl.BlockSpec((None, None, win // dil, None, 2, HEADS_PER_GROUP, HEAD_DIM),
                                      lambda s: (layer, s, 0, 0, 0, 0, 0)))
    f32_row = lambda wd: jax.ShapeDtypeStruct((DEC_BATCH, 1, wd), F32)
    outs = pl.pallas_call(
        _mix_sample_body,
        grid=(DEC_BATCH,),
        in_specs=[row(QKV_W), row(X_W), row(CONV_CH), *win_specs,
                  pl.BlockSpec((None, None, hist, CONV_CH), lambda s: (layer, s, 0, 0)),
                  pl.BlockSpec((None, None, N_MEM, 2, X_HEADS, X_HEAD_DIM), lambda s: (layer, s, 0, 0, 0, 0)),
                  pl.BlockSpec((CONV_WIDTH, CONV_CH), lambda s: (0, 0)), vec, vec, vec],
        out_specs=[row(ATT_OUT)] * 3 + [row(LANES)] * 3 + [row(CONV_CH), row(X_W),
                                                            pl.BlockSpec((None, hist, CONV_CH), lambda s: (s, 0, 0))],
        out_shape=[f32_row(ATT_OUT)] * 3 + [f32_row(LANES)] * 3 + [
            f32_row(CONV_CH), f32_row(X_W), jax.ShapeDtypeStruct((DEC_BATCH, hist, CONV_CH), F32)],
        compiler_params=_params("parallel"),
        name="mix_sample",
    )(qkv.reshape(DEC_BATCH, 1, QKV_W), qx.reshape(DEC_BATCH, 1, X_W), u.reshape(DEC_BATCH, 1, CONV_CH),
      *win_args, conv_state, mem_kv, w, b, g, bl)
    flat = [o.reshape(DEC_BATCH, -1) for o in outs[:8]]
    return flat[0:3], flat[3:6], flat[6].astype(BF16), flat[7], outs[8]


def _shift_window_body(x_ref, nxt_ref, new_ref, o_ref, *, n_c):
    rows = x_ref.shape[0]
    o_ref[0:rows - 1] = x_ref[1:rows]
    last = pl.program_id(2) == n_c - 1
    o_ref[rows - 1:rows] = jnp.where(last, new_ref[...], nxt_ref[...])


def _shift_window(state, new):
    rows = state.shape[2]
    rc = min(rows, SHIFT_ROWS)
    n_c = rows // rc
    tail = state.shape[3:]
    zeros = (0,) * len(tail)
    one = pl.BlockSpec((None, None, 1) + tail, lambda l, b, c: (l, b, jnp.minimum((c + 1) * rc, rows - 1)) + zeros)
    return pl.pallas_call(
        functools.partial(_shift_window_body, n_c=n_c),
        grid=(DEPTH, DEC_BATCH, n_c),
        in_specs=[pl.BlockSpec((None, None, rc) + tail, lambda l, b, c: (l, b, c) + zeros),
                  one,
                  pl.BlockSpec((None, None, 1) + tail, lambda l, b, c: (l, b, 0) + zeros)],
        out_specs=pl.BlockSpec((None, None, rc) + tail, lambda l, b, c: (l, b, c) + zeros),
        out_shape=jax.ShapeDtypeStruct(state.shape, state.dtype),
        compiler_params=_params("parallel", "parallel", "arbitrary"),
        name="shift_window",
    )(state, state, new)


def kernel(x_prompt, x_sample, state_win0, state_win1, state_win2, state_conv, cache_mem_kv, mem_prompt, w_ffn1_norm, w_ffn1_gate, w_ffn1_up, w_ffn1_down, w_mix_norm, w_in, w_dw, b_dw, g_cln, b_cln, w_conv_out, w_att_out, w_x_out, w_o, w_mem_norm, w_mem_kv, w_ffn2_norm, w_ffn2_gate, w_ffn2_up, w_ffn2_down, w_final_norm):
    state_wins = (state_win0, state_win1, state_win2)
    hp = x_prompt.reshape(BATCH * SEQ, D_MODEL)
    hs = x_sample.reshape(DEC_BATCH, D_MODEL)
    mem = mem_prompt.reshape(BATCH * N_MEM, D_MODEL)
    tab_p = _rope_tables(jnp.arange(SEQ))
    tab_s = _rope_tables(jnp.full((DEC_BATCH,), PAST_LEN))
    vec = lambda a: a.reshape(1, -1)
    g_final = vec(w_final_norm)
    ffn1_w = tuple(a.astype(BF16) for a in (w_ffn1_gate, w_ffn1_up, w_ffn1_down))
    ffn2_w = tuple(a.astype(BF16) for a in (w_ffn2_gate, w_ffn2_up, w_ffn2_down))
    w_in_bf, w_mem_kv_bf = w_in.astype(BF16), w_mem_kv.astype(BF16)

    win_p = [[] for _ in GROUPS]
    kv_s = [[] for _ in GROUPS]
    conv_p, conv_s, mem_p = [], [], []
    for l in range(DEPTH):
        last = l == DEPTH - 1
        conv_w = (w_dw[l], vec(b_dw[l]), vec(g_cln[l]), vec(b_cln[l]))
        out_w = tuple(a[l].astype(BF16) for a in (w_conv_out, w_att_out, w_x_out, w_o))

        mkv = _mem_proj(l, mem, vec(w_mem_norm[l]), w_mem_kv_bf)
        mem_p.append(mkv.reshape(BATCH, N_MEM, 2, X_HEADS, X_HEAD_DIM))

        hp, hs = _ffn(l, hp, hs, vec(w_ffn1_norm[l]), *ffn1_w)
        (qkv, u, qx, sg), (qkv_s, u_s, qx_s, sg_s) = _in_proj(l, hp, hs, vec(w_mix_norm[l]), w_in_bf, tab_p, tab_s)

        att = [_attn_prompt(qkv, gi) for gi in range(N_GROUPS)]
        c_act, conv_tail = _conv_prompt(u, *conv_w)
        o_mem = _cross_prompt(qx, mkv, tq=1024)
        hp = _post(c_act, [a[0] for a in att], [a[1] for a in att], o_mem, sg, *out_w, hp, tm=256)
        qkv3 = qkv.reshape(BATCH, SEQ, QKV_W)
        for gi, (win, _) in enumerate(GROUPS):
            keep = min(win, SEQ)
            cut = lambda c0: qkv3[:, SEQ - keep:, c0 + gi * ATT_OUT:c0 + (gi + 1) * ATT_OUT].reshape(
                BATCH, keep, HEADS_PER_GROUP, HEAD_DIM)
            win_p[gi].append(jnp.stack([cut(ATT_W), cut(2 * ATT_W)], axis=2))
        conv_p.append(conv_tail)

        o_g, lse_g, c_act, o_mem, conv_new = _mix_sample(l, qkv_s, qx_s, u_s, state_wins, state_conv, cache_mem_kv,
                                                         *conv_w)
        hs = _post(c_act, o_g, lse_g, o_mem, sg_s, *out_w, hs, tm=DEC_BATCH)
        for gi in range(N_GROUPS):
            cut = lambda c0: qkv_s[:, c0 + gi * ATT_OUT:c0 + (gi + 1) * ATT_OUT].reshape(
                DEC_BATCH, HEADS_PER_GROUP, HEAD_DIM)
            kv_s[gi].append(jnp.stack([cut(ATT_W), cut(2 * ATT_W)], axis=1))
        conv_s.append(conv_new)

        hp, hs = _ffn(l, hp, hs, vec(w_ffn2_norm[l]), *ffn2_w, g_final if last else None)

    st = lambda xs: jnp.stack(xs, axis=0)
    win_s = [_shift_window(s, st(k)[:, :, None]) for s, k in zip(state_wins, kv_s)]
    return (hp.reshape(BATCH, SEQ, D_MODEL), hs.reshape(DEC_BATCH, 1, D_MODEL),
            st(win_p[0]), st(win_p[1]), st(win_p[2]), st(conv_p), st(mem_p),
            win_s[0], win_s[1], win_s[2], st(conv_s))
```
